```python
import math
import jax, jax.numpy as jnp
from jax import lax
import numpy as np

D_MODEL = 1024
BATCH = 8
SEQ = 4096
DEPTH = 4

N_A = DEPTH // 2
N_B = DEPTH - N_A
RET_HEADS = 4
RET_QK_DIM = D_MODEL // RET_HEADS
RET_V_DIM = 2 * RET_QK_DIM
RET_V_TOTAL = RET_HEADS * RET_V_DIM
RET_CHUNK = 128
A_IN_WIDTH = 2 * D_MODEL + 2 * RET_V_TOTAL
SWA_HEAD_DIM = 64
SWA_Q_HEADS = D_MODEL // SWA_HEAD_DIM
SWA_KV_HEADS = SWA_Q_HEADS // 8
SWA_GROUP = SWA_Q_HEADS // SWA_KV_HEADS
SWA_Q_WIDTH = SWA_Q_HEADS * SWA_HEAD_DIM
KV_WIDTH = SWA_KV_HEADS * SWA_HEAD_DIM
WINDOW = 128
BLOCK = 128
REL_BUCKETS = 32
REL_MAX_DIST = 128
EPS = 1e-6

kernel_name = "yoco_retention_swa_sink_hybrid"


def rms(x):
    xf = x.astype(jnp.float32)
    return (xf * lax.rsqrt(jnp.mean(xf * xf, axis=-1, keepdims=True) + EPS)).astype(x.dtype)


def rmsnorm(x, g):
    return (rms(x) * g).astype(x.dtype)


def xpos_tables(seq, dim):
    angle = 1.0 / (10000.0 ** jnp.linspace(0.0, 1.0, dim // 2, dtype=jnp.float32))
    angle = jnp.repeat(angle, 2)
    pos = jnp.arange(seq, dtype=jnp.float32)[:, None]
    return jnp.sin(pos * angle), jnp.cos(pos * angle)


def theta_shift(x, sin, cos):
    x1 = x[..., ::2]
    x2 = x[..., 1::2]
    rot = jnp.stack([-x2, x1], axis=-1).reshape(x.shape)
    return x * cos[:, None, :].astype(x.dtype) + rot * sin[:, None, :].astype(x.dtype)


def retention(q, k, v):
    b, s, h, dk = q.shape
    dv = v.shape[-1]
    c = RET_CHUNK
    n = s // c
    dt = q.dtype
    log_gamma = jnp.log(1.0 - 2.0 ** (-5.0 - jnp.arange(h, dtype=jnp.float32)))
    idx = jnp.arange(c, dtype=jnp.float32)
    diff = idx[:, None] - idx[None, :]
    intra_decay = jnp.where(diff[None] >= 0,
                            jnp.exp(jnp.maximum(diff, 0.0)[None] * log_gamma[:, None, None]),
                            0.0).astype(dt)
    q_decay = jnp.exp((idx + 1.0)[:, None] * log_gamma[None, :]).astype(dt)
    k_decay = jnp.exp((c - 1.0 - idx)[:, None] * log_gamma[None, :]).astype(dt)
    chunk_decay = jnp.exp(c * log_gamma).astype(dt)
    k = k * jnp.asarray(dk ** -0.5, dt)

    def to_chunks(t):
        return t.reshape(b, n, c, h, t.shape[-1]).transpose(1, 0, 2, 3, 4)

    def step(state, inp):
        qc, kc, vc = inp
        scores = jnp.einsum('bihd,bjhd->bhij', qc, kc) * intra_decay[None]
        intra = jnp.einsum('bhij,bjhe->bihe', scores, vc)
        inter = jnp.einsum('bihd,bhde->bihe', qc, state) * q_decay[None, :, :, None]
        new_state = (state * chunk_decay[None, :, None, None]
                     + jnp.einsum('bjhd,bjhe->bhde', kc * k_decay[None, :, :, None], vc))
        return new_state, intra + inter

    state0 = jnp.zeros((b, h, dk, dv), dt)
    _, out = lax.scan(step, state0, (to_chunks(q), to_chunks(k), to_chunks(v)))
    return out.transpose(1, 0, 2, 3, 4).reshape(b, s, h, dv)


def retention_layer(x, g_norm, w_in, w_out, sin, cos):
    b, s, _ = x.shape
    proj = rmsnorm(x, g_norm) @ w_in
    q, k, v, gate = jnp.split(proj, [D_MODEL, 2 * D_MODEL, 2 * D_MODEL + RET_V_TOTAL], axis=-1)
    q = theta_shift(q.reshape(b, s, RET_HEADS, RET_QK_DIM), sin, cos)
    k = theta_shift(k.reshape(b, s, RET_HEADS, RET_QK_DIM), sin, cos)
    v = v.reshape(b, s, RET_HEADS, RET_V_DIM)
    o = rms(retention(q, k, v))
    o = o.reshape(b, s, RET_V_TOTAL) * jax.nn.silu(gate)
    return x + o @ w_out


def shared_kv(x, g_norm, w_kv, g_k):
    b, s, _ = x.shape
    nb = s // BLOCK
    kv = rmsnorm(x, g_norm) @ w_kv
    k, v = jnp.split(kv, [KV_WIDTH], axis=-1)
    k = rmsnorm(k.reshape(b, s, SWA_KV_HEADS, SWA_HEAD_DIM), g_k)
    v = v.reshape(b, s, SWA_KV_HEADS, SWA_HEAD_DIM)

    def band(t):
        tb = t.reshape(b, nb, BLOCK, SWA_KV_HEADS, SWA_HEAD_DIM)
        prev = jnp.concatenate([jnp.zeros_like(tb[:, :1]), tb[:, :-1]], axis=1)
        return jnp.concatenate([prev, tb], axis=2)

    return band(k), band(v)


def t5_causal_bucket(dist):
    max_exact = REL_BUCKETS // 2
    dist_f = jnp.maximum(dist, 1).astype(jnp.float32)
    large = max_exact + (jnp.log(dist_f / max_exact) / math.log(REL_MAX_DIST / max_exact)
                         * (REL_BUCKETS - max_exact)).astype(jnp.int32)
    large = jnp.minimum(large, REL_BUCKETS - 1)
    return jnp.where(dist < max_exact, dist, large)


def band_bias_and_mask(rel_bias, nb):
    i = jnp.arange(BLOCK)[:, None]
    j = jnp.arange(2 * BLOCK)[None, :]
    dist = i + BLOCK - j
    bucket = t5_causal_bucket(jnp.maximum(dist, 0))
    bias = rel_bias.astype(jnp.float32)[bucket]
    bias = bias.transpose(2, 0, 1).reshape(SWA_KV_HEADS, SWA_GROUP, BLOCK, 2 * BLOCK)
    in_window = (dist >= 0) & (dist < WINDOW)
    key_pos = jnp.arange(nb)[:, None, None] * BLOCK - BLOCK + j[None]
    mask = in_window[None] & (key_pos >= 0)
    return bias, mask


def swa_layer(x, g_norm, w_in, g_q, sinks, w_out, k_band, v_band, bias, mask):
    b, s, _ = x.shape
    nb = s // BLOCK
    proj = rmsnorm(x, g_norm) @ w_in
    q, gate = jnp.split(proj, [SWA_Q_WIDTH], axis=-1)
    q = rmsnorm(q.reshape(b, nb, BLOCK, SWA_KV_HEADS, SWA_GROUP, SWA_HEAD_DIM), g_q)
    scores = jnp.einsum('bnqkgd,bnjkd->bnkgqj', q, k_band).astype(jnp.float32)
    scores = scores * (SWA_HEAD_DIM ** -0.5) + bias[None, None]
    scores = jnp.where(mask[None, :, None, None], scores, -jnp.inf)
    sink = sinks.astype(jnp.float32).reshape(1, 1, SWA_KV_HEADS, SWA_GROUP, 1, 1)
    m = jnp.maximum(jnp.max(scores, axis=-1, keepdims=True), sink)
    p = jnp.exp(scores - m)
    p = p / (jnp.sum(p, axis=-1, keepdims=True) + jnp.exp(sink - m))
    o = jnp.einsum('bnkgqj,bnjkd->bnqkgd', p.astype(v_band.dtype), v_band)
    o = o.reshape(b, s, SWA_Q_WIDTH) * jax.nn.silu(gate)
    return x + o @ w_out


def setup_inputs(seed: int = 0) -> dict:
    key = jax.random.key(seed)
    ks = jax.random.split(key, 13)

    def nrm(k, shape, scale):
        return jax.random.normal(k, shape, jnp.float32) * scale

    return {
        "x": nrm(ks[0], (BATCH, SEQ, D_MODEL), 1.0),
        "a_norm_g": 1.0 + nrm(ks[1], (N_A, D_MODEL), 0.02),
        "a_w_in": nrm(ks[2], (N_A, D_MODEL, A_IN_WIDTH), D_MODEL ** -0.5),
        "a_w_out": nrm(ks[3], (N_A, RET_V_TOTAL, D_MODEL), RET_V_TOTAL ** -0.5),
        "kv_norm_g": 1.0 + nrm(ks[4], (D_MODEL,), 0.02),
        "w_kv": nrm(ks[5], (D_MODEL, 2 * KV_WIDTH), D_MODEL ** -0.5),
        "k_norm_g": 1.0 + nrm(ks[6], (SWA_HEAD_DIM,), 0.02),
        "rel_bias": nrm(ks[7], (REL_BUCKETS, SWA_Q_HEADS), 0.5),
        "b_norm_g": 1.0 + nrm(ks[8], (N_B, D_MODEL), 0.02),
        "b_w_in": nrm(ks[9], (N_B, D_MODEL, 2 * SWA_Q_WIDTH), D_MODEL ** -0.5),
        "b_q_norm_g": 1.0 + nrm(ks[10], (N_B, SWA_HEAD_DIM), 0.02),
        "b_sinks": nrm(ks[11], (N_B, SWA_Q_HEADS), 1.0),
        "b_w_out": nrm(ks[12], (N_B, SWA_Q_WIDTH, D_MODEL), SWA_Q_WIDTH ** -0.5),
    }


def reference(x, a_norm_g, a_w_in, a_w_out, kv_norm_g, w_kv, k_norm_g, rel_bias,
              b_norm_g, b_w_in, b_q_norm_g, b_sinks, b_w_out):
    s = x.shape[1]
    sin, cos = xpos_tables(s, RET_QK_DIM)
    bias, mask = band_bias_and_mask(rel_bias, s // BLOCK)
    k_band = None
    v_band = None
    for layer in range(DEPTH):
        if layer < N_A:
            x = retention_layer(x, a_norm_g[layer], a_w_in[layer], a_w_out[layer], sin, cos)
        else:
            j = layer - N_A
            if j == 0:
                k_band, v_band = shared_kv(x, kv_norm_g, w_kv, k_norm_g)
            x = swa_layer(x, b_norm_g[j], b_w_in[j], b_q_norm_g[j], b_sinks[j], b_w_out[j],
                          k_band, v_band, bias, mask)
    return x
```

```python
import functools
import math

import jax
import jax.numpy as jnp
from jax import lax
from jax.experimental import pallas as pl
from jax.experimental.pallas import tpu as pltpu

D_MODEL = 1024
RET_HEADS = 4
RET_QK_DIM = D_MODEL // RET_HEADS
RET_V_DIM = 2 * RET_QK_DIM
RET_V_TOTAL = RET_HEADS * RET_V_DIM
RET_CHUNK = 128
SWA_HEAD_DIM = 64
SWA_Q_HEADS = D_MODEL // SWA_HEAD_DIM
SWA_KV_HEADS = SWA_Q_HEADS // 8
SWA_Q_WIDTH = SWA_Q_HEADS * SWA_HEAD_DIM
KV_WIDTH = SWA_KV_HEADS * SWA_HEAD_DIM
WINDOW = 128
BLOCK = 128
REL_BUCKETS = 32
REL_MAX_DIST = 128
EPS = 1e-6

LANES = 128
VMEM_LIMIT_BYTES = 56 * 1024 * 1024

RET_TILE = 256
KV_TILE = 512

BF16 = jnp.bfloat16
F32 = jnp.float32


def _dot(a, b):
    return jnp.dot(a, b, preferred_element_type=F32)


def _dot_nt(a, b):
    return lax.dot_general(a, b, (((1,), (1,)), ((), ())), preferred_element_type=F32)


def _dot_tn(a, b):
    return lax.dot_general(a, b, (((0,), (0,)), ((), ())), preferred_element_type=F32)


def _rms(x):
    return x * lax.rsqrt(jnp.mean(x * x, axis=-1, keepdims=True) + EPS)


def _resident(shape):
    nd = len(shape)
    return pl.BlockSpec(shape, lambda *_: (0,) * nd, pipeline_mode=pl.Buffered(1))


def _ret_kernel(x_ref, g_ref, wq_ref, wk_ref, wv_ref, wg_ref, wout_ref, cos_ref, sin_ref,
                idec_ref, qdec_ref, kdec_ref, cdec_ref, o_ref, state_ref):
    @pl.when(pl.program_id(1) == 0)
    def _():
        state_ref[...] = jnp.zeros_like(state_ref)

    x = x_ref[0]
    h = (_rms(x) * g_ref[...]).astype(BF16)
    cos = cos_ref[...]
    sin = sin_ref[...]
    half = RET_QK_DIM // 2
    n_chunks = x.shape[0] // RET_CHUNK

    def rotate(t):
        t1, t2 = t[:, :half], t[:, half:]
        return jnp.concatenate([t1 * cos - t2 * sin, t2 * cos + t1 * sin], axis=1)

    acc = x
    for hd in range(RET_HEADS):
        q = rotate(_dot(h, wq_ref[hd]))
        k = rotate(_dot(h, wk_ref[hd])) * (RET_QK_DIM ** -0.5)
        v = _dot(h, wv_ref[hd]).astype(BF16)
        qb = q.astype(BF16)
        kb = k.astype(BF16)
        outs = []
        for c in range(n_chunks):
            rows = slice(c * RET_CHUNK, (c + 1) * RET_CHUNK)
            qc, kc, vc = qb[rows], kb[rows], v[rows]
            st = state_ref[hd]
            scores = _dot_nt(qc, kc) * idec_ref[hd]
            intra = _dot(scores.astype(BF16), vc)
            inter = _dot(qc, st.astype(BF16)) * qdec_ref[hd]
            outs.append(intra + inter)
            kd = (k[rows] * kdec_ref[hd]).astype(BF16)
            state_ref[hd] = st * cdec_ref[hd] + _dot_tn(kd, vc)
        o = _rms(jnp.concatenate(outs, axis=0))
        gate = _dot(h, wg_ref[hd])
        y = (o * (gate * jax.nn.sigmoid(gate))).astype(BF16)
        acc = acc + _dot(y, wout_ref[hd])
    o_ref[0] = acc


def _retention_layer(x, g, w_in, w_out, cos, sin, idec, qdec, kdec, cdec):
    b, s, d = x.shape
    perm = jnp.concatenate([jnp.arange(0, RET_QK_DIM, 2), jnp.arange(1, RET_QK_DIM, 2)])

    def heads(w, width):
        return w.reshape(d, RET_HEADS, width).transpose(1, 0, 2)

    wq = heads(w_in[:, :D_MODEL], RET_QK_DIM)[:, :, perm].astype(BF16)
    wk = heads(w_in[:, D_MODEL:2 * D_MODEL], RET_QK_DIM)[:, :, perm].astype(BF16)
    wv = heads(w_in[:, 2 * D_MODEL:2 * D_MODEL + RET_V_TOTAL], RET_V_DIM).astype(BF16)
    wg = heads(w_in[:, 2 * D_MODEL + RET_V_TOTAL:], RET_V_DIM).astype(BF16)
    wo = w_out.reshape(RET_HEADS, RET_V_DIM, d).astype(BF16)

    t = RET_TILE
    half = RET_QK_DIM // 2
    return pl.pallas_call(
        _ret_kernel,
        grid=(b, s // t),
        in_specs=[
            pl.BlockSpec((1, t, d), lambda i, j: (i, j, 0)),
            _resident((1, d)),
            _resident(wq.shape), _resident(wk.shape), _resident(wv.shape),
            _resident(wg.shape), _resident(wo.shape),
            pl.BlockSpec((t, half), lambda i, j: (j, 0)),
            pl.BlockSpec((t, half), lambda i, j: (j, 0)),
            _resident(idec.shape), _resident(qdec.shape), _resident(kdec.shape),
            _resident(cdec.shape),
        ],
        out_specs=pl.BlockSpec((1, t, d), lambda i, j: (i, j, 0)),
        out_shape=jax.ShapeDtypeStruct(x.shape, x.dtype),
        scratch_shapes=[pltpu.VMEM((RET_HEADS, RET_QK_DIM, RET_V_DIM), F32)],
        compiler_params=pltpu.CompilerParams(
            dimension_semantics=("parallel", "arbitrary"),
            vmem_limit_bytes=VMEM_LIMIT_BYTES),
        name="retention_layer",
    )(x, g.reshape(1, d), wq, wk, wv, wg, wo, cos, sin, idec, qdec, kdec, cdec)


def _retention_tables(s):
    dk = RET_QK_DIM
    angle = 1.0 / (10000.0 ** jnp.linspace(0.0, 1.0, dk // 2, dtype=F32))
    pos = jnp.arange(s, dtype=F32)[:, None]
    sin, cos = jnp.sin(pos * angle), jnp.cos(pos * angle)
    c = RET_CHUNK
    log_gamma = jnp.log(1.0 - 2.0 ** (-5.0 - jnp.arange(RET_HEADS, dtype=F32)))
    idx = jnp.arange(c, dtype=F32)
    diff = idx[:, None] - idx[None, :]
    idec = jnp.where(diff[None] >= 0,
                     jnp.exp(jnp.maximum(diff, 0.0)[None] * log_gamma[:, None, None]), 0.0)
    qdec = jnp.exp((idx + 1.0)[None, :, None] * log_gamma[:, None, None])
    kdec = jnp.exp((c - 1.0 - idx)[None, :, None] * log_gamma[:, None, None])
    cdec = jnp.exp(c * log_gamma)[:, None, None]
    return cos, sin, idec.astype(F32), qdec, kdec, cdec


def _kv_kernel(x_ref, g_ref, w_ref, gk_ref, k_ref, v_ref):
    h = (_rms(x_ref[0]) * g_ref[...]).astype(BF16)
    kv = _dot(h, w_ref[...])
    k, v = kv[:, :KV_WIDTH], kv[:, KV_WIDTH:]
    lo = lax.broadcasted_iota(jnp.int32, (1, KV_WIDTH), 1) < SWA_HEAD_DIM
    sq = k * k
    ms_lo = jnp.sum(jnp.where(lo, sq, 0.0), axis=-1, keepdims=True) / SWA_HEAD_DIM
    ms_hi = jnp.sum(jnp.where(lo, 0.0, sq), axis=-1, keepdims=True) / SWA_HEAD_DIM
    r = jnp.where(lo, lax.rsqrt(ms_lo + EPS), lax.rsqrt(ms_hi + EPS))
    k_ref[0] = k * r * gk_ref[...]
    v_ref[0] = v


def _shared_kv(x, g, w_kv, g_k):
    b, s, d = x.shape
    t = KV_TILE
    out = jax.ShapeDtypeStruct((b, s, KV_WIDTH), F32)
    return pl.pallas_call(
        _kv_kernel,
        grid=(b, s // t),
        in_specs=[
            pl.BlockSpec((1, t, d), lambda i, j: (i, j, 0)),
            _resident((1, d)),
            _resident(w_kv.shape),
            _resident((1, KV_WIDTH)),
        ],
        out_specs=[pl.BlockSpec((1, t, KV_WIDTH), lambda i, j: (i, j, 0))] * 2,
        out_shape=[out, out],
        compiler_params=pltpu.CompilerParams(
            dimension_semantics=("parallel", "parallel"),
            vmem_limit_bytes=VMEM_LIMIT_BYTES),
        name="shared_kv",
    )(x, g.reshape(1, d), w_kv.astype(BF16), jnp.tile(g_k, SWA_KV_HEADS).reshape(1, KV_WIDTH))


def _bias_kernel(rel_ref, o_ref):
    hd = pl.program_id(0)
    i = lax.broadcasted_iota(jnp.int32, (BLOCK, 2 * BLOCK), 0)
    j = lax.broadcasted_iota(jnp.int32, (BLOCK, 2 * BLOCK), 1)
    dist = i + BLOCK - j
    in_window = (dist >= 0) & (dist < WINDOW)
    dist = jnp.maximum(dist, 0)
    max_exact = REL_BUCKETS // 2
    dist_f = jnp.maximum(dist, 1).astype(F32)
    large = max_exact + (jnp.log(dist_f / max_exact) / math.log(REL_MAX_DIST / max_exact)
                         * (REL_BUCKETS - max_exact)).astype(jnp.int32)
    large = jnp.minimum(large, REL_BUCKETS - 1)
    bucket = jnp.where(dist < max_exact, dist, large)
    bias = jnp.zeros((BLOCK, 2 * BLOCK), F32)
    for bk in range(REL_BUCKETS):
        bias = jnp.where(bucket == bk, rel_ref[bk, hd], bias)
    neg = jnp.float32(-jnp.inf)
    o_ref[0, 0] = jnp.where(in_window, bias, neg)
    o_ref[1, 0] = jnp.where(in_window & (j >= BLOCK), bias, neg)


def _bias_tables(rel_bias):
    return pl.pallas_call(
        _bias_kernel,
        grid=(SWA_Q_HEADS,),
        in_specs=[pl.BlockSpec(memory_space=pltpu.SMEM)],
        out_specs=pl.BlockSpec((2, 1, BLOCK, 2 * BLOCK), lambda h: (0, h, 0, 0)),
        out_shape=jax.ShapeDtypeStruct((2, SWA_Q_HEADS, BLOCK, 2 * BLOCK), F32),
        name="rel_bias_tables",
    )(rel_bias)


def _swa_kernel(sinks_ref, x_ref, g_ref, wq_ref, wg_ref, gq_ref, wout_ref,
                kc_ref, kp_ref, vc_ref, vp_ref, bias_ref, o_ref):
    first = (pl.program_id(1) == 0).astype(jnp.int32)
    x = x_ref[0]
    h = (_rms(x) * g_ref[...]).astype(BF16)
    q = _dot(h, wq_ref[...])
    gate = _dot(h, wg_ref[...])

    lo = lax.broadcasted_iota(jnp.int32, (1, LANES), 1) < SWA_HEAD_DIM

    def placements(cur_ref, prev_ref):
        band = jnp.concatenate([prev_ref[0], cur_ref[0]], axis=0)
        swapped = pltpu.roll(band, SWA_HEAD_DIM, 1)
        z = jnp.zeros_like(band)
        return [[jnp.where(lo, band, z).astype(BF16), jnp.where(lo, z, swapped).astype(BF16)],
                [jnp.where(lo, swapped, z).astype(BF16), jnp.where(lo, z, band).astype(BF16)]]

    kvar = placements(kc_ref, kp_ref)
    vvar = placements(vc_ref, vp_ref)
    gq = gq_ref[...]

    pieces = []
    for p in range(SWA_Q_HEADS // 2):
        kv = (2 * p) // (SWA_Q_HEADS // SWA_KV_HEADS)
        qp = q[:, p * LANES:(p + 1) * LANES]
        sq = qp * qp
        ms_lo = jnp.sum(jnp.where(lo, sq, 0.0), axis=-1, keepdims=True) / SWA_HEAD_DIM
        ms_hi = jnp.sum(jnp.where(lo, 0.0, sq), axis=-1, keepdims=True) / SWA_HEAD_DIM
        r = jnp.where(lo, lax.rsqrt(ms_lo + EPS), lax.rsqrt(ms_hi + EPS))
        qn = (qp * r * gq).astype(BF16)
        o_pair = None
        for pos in range(2):
            hd = 2 * p + pos
            sink = sinks_ref[hd]
            s = _dot_nt(qn, kvar[kv][pos]) * (SWA_HEAD_DIM ** -0.5) + bias_ref[first, hd]
            m = jnp.maximum(jnp.max(s, axis=-1, keepdims=True), sink)
            e = jnp.exp(s - m)
            den = jnp.sum(e, axis=-1, keepdims=True) + jnp.exp(sink - m)
            pn = (e / den).astype(BF16)
            part = _dot(pn, vvar[kv][pos])
            o_pair = part if o_pair is None else o_pair + part
        pieces.append(o_pair)
    o = jnp.concatenate(pieces, axis=1)
    y = (o * (gate * jax.nn.sigmoid(gate))).astype(BF16)
    o_ref[0] = x + _dot(y, wout_ref[...])


def _swa_layer(x, g, w_in, g_q, sinks, w_out, k, v, bias):
    b, s, d = x.shape
    t = BLOCK
    wq = w_in[:, :SWA_Q_WIDTH].astype(BF16)
    wg = w_in[:, SWA_Q_WIDTH:].astype(BF16)
    cur = pl.BlockSpec((1, t, KV_WIDTH), lambda i, j: (i, j, 0))
    prev = pl.BlockSpec((1, t, KV_WIDTH), lambda i, j: (i, jnp.maximum(j - 1, 0), 0))
    return pl.pallas_call(
        _swa_kernel,
        grid=(b, s // t),
        in_specs=[
            pl.BlockSpec(memory_space=pltpu.SMEM),
            pl.BlockSpec((1, t, d), lambda i, j: (i, j, 0)),
            _resident((1, d)),
            _resident(wq.shape), _resident(wg.shape),
            _resident((1, LANES)),
            _resident(w_out.shape),
            cur, prev, cur, prev,
            _resident(bias.shape),
        ],
        out_specs=pl.BlockSpec((1, t, d), lambda i, j: (i, j, 0)),
        out_shape=jax.ShapeDtypeStruct(x.shape, x.dtype),
        compiler_params=pltpu.CompilerParams(
            dimension_semantics=("parallel", "parallel"),
            vmem_limit_bytes=VMEM_LIMIT_BYTES),
        name="swa_layer",
    )(sinks, x, g.reshape(1, d), wq, wg, jnp.tile(g_q, 2).reshape(1, LANES),
      w_out.astype(BF16), k, k, v, v, bias)


def kernel(x, a_norm_g, a_w_in, a_w_out, kv_norm_g, w_kv, k_norm_g, rel_bias,
           b_norm_g, b_w_in, b_q_norm_g, b_sinks, b_w_out):
    s = x.shape[1]
    tables = _retention_tables(s)
    for layer in range(a_w_in.shape[0]):
        x = _retention_layer(x, a_norm_g[layer], a_w_in[layer], a_w_out[layer], *tables)
    k, v = _shared_kv(x, kv_norm_g, w_kv, k_norm_g)
    bias = _bias_tables(rel_bias)
    for j in range(b_w_in.shape[0]):
        x = _swa_layer(x, b_norm_g[j], b_w_in[j], b_q_norm_g[j], b_sinks[j], b_w_out[j],
                       k, v, bias)
    return x
```

```python
import functools
import math

import jax
import jax.numpy as jnp
from jax import lax
from jax.experimental import pallas as pl
from jax.experimental.pallas import tpu as pltpu

D_MODEL = 1024
RET_HEADS = 4
RET_QK_DIM = D_MODEL // RET_HEADS
RET_V_DIM = 2 * RET_QK_DIM
RET_V_TOTAL = RET_HEADS * RET_V_DIM
RET_CHUNK = 128
SWA_HEAD_DIM = 64
SWA_Q_HEADS = D_MODEL // SWA_HEAD_DIM
SWA_KV_HEADS = SWA_Q_HEADS // 8
SWA_Q_WIDTH = SWA_Q_HEADS * SWA_HEAD_DIM
KV_WIDTH = SWA_KV_HEADS * SWA_HEAD_DIM
WINDOW = 128
BLOCK = 128
REL_BUCKETS = 32
REL_MAX_DIST = 128
EPS = 1e-6

LANES = 128
VMEM_LIMIT_BYTES = 56 * 1024 * 1024

RET_TILE = 256
KV_TILE = 512
SWA_TILE = 512

BF16 = jnp.bfloat16
F32 = jnp.float32


def _dot(a, b):
    return jnp.dot(a, b, preferred_element_type=F32)


def _dot_nt(a, b):
    return lax.dot_general(a, b, (((1,), (1,)), ((), ())), preferred_element_type=F32)


def _dot_tn(a, b):
    return lax.dot_general(a, b, (((0,), (0,)), ((), ())), preferred_element_type=F32)


SLAB = 256


def _slabs(w):
    k, n = w.shape
    return w.reshape(k, n // SLAB, SLAB).transpose(1, 0, 2).astype(BF16)


def _dot_slabs(a, w_ref):
    return jnp.concatenate([_dot(a, w_ref[c]) for c in range(w_ref.shape[0])], axis=1)


def _rms(x):
    return x * lax.rsqrt(jnp.mean(x * x, axis=-1, keepdims=True) + EPS)


def _resident(shape):
    nd = len(shape)
    return pl.BlockSpec(shape, lambda *_: (0,) * nd, pipeline_mode=pl.Buffered(1))


def _ret_kernel(x_ref, g_ref, wq_ref, wk_ref, wv_ref, wg_ref, wout_ref, cos_ref, sin_ref,
                idec_ref, qdec_ref, kdec_ref, cdec_ref, o_ref, state_ref):
    @pl.when(pl.program_id(1) == 0)
    def _():
        state_ref[...] = jnp.zeros_like(state_ref)

    x = x_ref[0]
    h = (_rms(x) * g_ref[...]).astype(BF16)
    cos = cos_ref[...]
    sin = sin_ref[...]
    half = RET_QK_DIM // 2
    n_chunks = x.shape[0] // RET_CHUNK

    def rotate(t):
        t1, t2 = t[:, :half], t[:, half:]
        return jnp.concatenate([t1 * cos - t2 * sin, t2 * cos + t1 * sin], axis=1)

    acc = x
    for hd in range(RET_HEADS):
        q = rotate(_dot(h, wq_ref[hd]))
        k = rotate(_dot(h, wk_ref[hd])) * (RET_QK_DIM ** -0.5)
        v = _dot(h, wv_ref[hd]).astype(BF16)
        qb = q.astype(BF16)
        kb = k.astype(BF16)
        outs = []
        for c in range(n_chunks):
            rows = slice(c * RET_CHUNK, (c + 1) * RET_CHUNK)
            qc, kc, vc = qb[rows], kb[rows], v[rows]
            st = state_ref[hd]
            scores = _dot_nt(qc, kc) * idec_ref[hd]
            intra = _dot(scores.astype(BF16), vc)
            inter = _dot(qc, st.astype(BF16)) * qdec_ref[hd]
            outs.append(intra + inter)
            kd = (k[rows] * kdec_ref[hd]).astype(BF16)
            state_ref[hd] = st * cdec_ref[hd] + _dot_tn(kd, vc)
        o = _rms(jnp.concatenate(outs, axis=0))
        gate = _dot(h, wg_ref[hd])
        y = (o * (gate * jax.nn.sigmoid(gate))).astype(BF16)
        acc = acc + _dot_slabs(y, wout_ref.at[hd])
    o_ref[0] = acc


def _retention_layer(x, g, w_in, w_out, cos, sin, idec, qdec, kdec, cdec):
    b, s, d = x.shape
    perm = jnp.concatenate([jnp.arange(0, RET_QK_DIM, 2), jnp.arange(1, RET_QK_DIM, 2)])

    def heads(w, width):
        return w.reshape(d, RET_HEADS, width).transpose(1, 0, 2)

    wq = heads(w_in[:, :D_MODEL], RET_QK_DIM)[:, :, perm].astype(BF16)
    wk = heads(w_in[:, D_MODEL:2 * D_MODEL], RET_QK_DIM)[:, :, perm].astype(BF16)
    wv = heads(w_in[:, 2 * D_MODEL:2 * D_MODEL + RET_V_TOTAL], RET_V_DIM).astype(BF16)
    wg = heads(w_in[:, 2 * D_MODEL + RET_V_TOTAL:], RET_V_DIM).astype(BF16)
    wo = jnp.stack([_slabs(w) for w in w_out.reshape(RET_HEADS, RET_V_DIM, d)])

    t = RET_TILE
    half = RET_QK_DIM // 2
    return pl.pallas_call(
        _ret_kernel,
        grid=(b, s // t),
        in_specs=[
            pl.BlockSpec((1, t, d), lambda i, j: (i, j, 0)),
            _resident((1, d)),
            _resident(wq.shape), _resident(wk.shape), _resident(wv.shape),
            _resident(wg.shape), _resident(wo.shape),
            pl.BlockSpec((t, half), lambda i, j: (j, 0)),
            pl.BlockSpec((t, half), lambda i, j: (j, 0)),
            _resident(idec.shape), _resident(qdec.shape), _resident(kdec.shape),
            _resident(cdec.shape),
        ],
        out_specs=pl.BlockSpec((1, t, d), lambda i, j: (i, j, 0)),
        out_shape=jax.ShapeDtypeStruct(x.shape, x.dtype),
        scratch_shapes=[pltpu.VMEM((RET_HEADS, RET_QK_DIM, RET_V_DIM), F32)],
        compiler_params=pltpu.CompilerParams(
            dimension_semantics=("parallel", "arbitrary"),
            vmem_limit_bytes=VMEM_LIMIT_BYTES),
        name="retention_layer",
    )(x, g.reshape(1, d), wq, wk, wv, wg, wo, cos, sin, idec, qdec, kdec, cdec)


def _retention_tables(s):
    dk = RET_QK_DIM
    angle = 1.0 / (10000.0 ** jnp.linspace(0.0, 1.0, dk // 2, dtype=F32))
    pos = jnp.arange(s, dtype=F32)[:, None]
    sin, cos = jnp.sin(pos * angle), jnp.cos(pos * angle)
    c = RET_CHUNK
    log_gamma = jnp.log(1.0 - 2.0 ** (-5.0 - jnp.arange(RET_HEADS, dtype=F32)))
    idx = jnp.arange(c, dtype=F32)
    diff = idx[:, None] - idx[None, :]
    idec = jnp.where(diff[None] >= 0,
                     jnp.exp(jnp.maximum(diff, 0.0)[None] * log_gamma[:, None, None]), 0.0)
    qdec = jnp.exp((idx + 1.0)[None, :, None] * log_gamma[:, None, None])
    kdec = jnp.exp((c - 1.0 - idx)[None, :, None] * log_gamma[:, None, None])
    cdec = jnp.exp(c * log_gamma)[:, None, None]
    return cos, sin, idec.astype(F32), qdec, kdec, cdec


def _kv_kernel(x_ref, g_ref, w_ref, gk_ref, k_ref, v_ref):
    h = (_rms(x_ref[0]) * g_ref[...]).astype(BF16)
    kv = _dot(h, w_ref[...])
    k, v = kv[:, :KV_WIDTH], kv[:, KV_WIDTH:]
    lo = lax.broadcasted_iota(jnp.int32, (1, KV_WIDTH), 1) < SWA_HEAD_DIM
    sq = k * k
    ms_lo = jnp.sum(jnp.where(lo, sq, 0.0), axis=-1, keepdims=True) / SWA_HEAD_DIM
    ms_hi = jnp.sum(jnp.where(lo, 0.0, sq), axis=-1, keepdims=True) / SWA_HEAD_DIM
    r = jnp.where(lo, lax.rsqrt(ms_lo + EPS), lax.rsqrt(ms_hi + EPS))
    k_ref[0] = k * r * gk_ref[...]
    v_ref[0] = v


def _shared_kv(x, g, w_kv, g_k):
    b, s, d = x.shape
    t = KV_TILE
    out = jax.ShapeDtypeStruct((b, s, KV_WIDTH), F32)
    return pl.pallas_call(
        _kv_kernel,
        grid=(b, s // t),
        in_specs=[
            pl.BlockSpec((1, t, d), lambda i, j: (i, j, 0)),
            _resident((1, d)),
            _resident(w_kv.shape),
            _resident((1, KV_WIDTH)),
        ],
        out_specs=[pl.BlockSpec((1, t, KV_WIDTH), lambda i, j: (i, j, 0))] * 2,
        out_shape=[out, out],
        compiler_params=pltpu.CompilerParams(
            dimension_semantics=("parallel", "parallel"),
            vmem_limit_bytes=VMEM_LIMIT_BYTES),
        name="shared_kv",
    )(x, g.reshape(1, d), w_kv.astype(BF16), jnp.tile(g_k, SWA_KV_HEADS).reshape(1, KV_WIDTH))


def _prev_key_valid():
    j = lax.broadcasted_iota(jnp.int32, (BLOCK, BLOCK), 0)
    i = lax.broadcasted_iota(jnp.int32, (BLOCK, BLOCK), 1)
    return j > i, j, i


def _bias_kernel(rel_ref, o_ref):
    hd = pl.program_id(0)
    prev, j, i = _prev_key_valid()
    dist = jnp.where(prev, i + BLOCK - j, i - j)
    max_exact = REL_BUCKETS // 2
    dist_f = jnp.maximum(dist, 1).astype(F32)
    large = max_exact + (jnp.log(dist_f / max_exact) / math.log(REL_MAX_DIST / max_exact)
                         * (REL_BUCKETS - max_exact)).astype(jnp.int32)
    large = jnp.minimum(large, REL_BUCKETS - 1)
    bucket = jnp.where(dist < max_exact, dist, large)
    bias = jnp.zeros((BLOCK, BLOCK), F32)
    for bk in range(REL_BUCKETS):
        bias = jnp.where(bucket == bk, rel_ref[bk, hd], bias)
    o_ref[0, 0] = bias
    o_ref[1, 0] = jnp.where(prev, jnp.float32(-jnp.inf), bias)


def _bias_tables(rel_bias):
    return pl.pallas_call(
        _bias_kernel,
        grid=(SWA_Q_HEADS,),
        in_specs=[pl.BlockSpec(memory_space=pltpu.SMEM)],
        out_specs=pl.BlockSpec((2, 1, BLOCK, BLOCK), lambda h: (0, h, 0, 0)),
        out_shape=jax.ShapeDtypeStruct((2, SWA_Q_HEADS, BLOCK, BLOCK), F32),
        name="rel_bias_tables",
    )(rel_bias)


def _swa_kernel(sinks_ref, x_ref, g_ref, wq_ref, wg_ref, gq_ref, wout_ref,
                kc_ref, kp_ref, vc_ref, vp_ref, bias_ref, o_ref):
    first = (pl.program_id(1) == 0).astype(jnp.int32)
    x = x_ref[0]
    n_blocks = x.shape[0] // BLOCK
    n_groups = SWA_Q_HEADS // 4
    h = (_rms(x) * g_ref[...]).astype(BF16)

    lo = lax.broadcasted_iota(jnp.int32, (1, LANES), 1) < SWA_HEAD_DIM

    kall = jnp.concatenate([kp_ref[0], kc_ref[0]], axis=0)
    kswap = pltpu.roll(kall, SWA_HEAD_DIM, 1)
    kdup = [jnp.where(lo, kall, kswap).astype(BF16), jnp.where(lo, kswap, kall).astype(BF16)]
    vall = jnp.concatenate([vp_ref[0], vc_ref[0]], axis=0)
    vswap = pltpu.roll(vall, SWA_HEAD_DIM, 1)
    zero = jnp.zeros_like(vall)
    v_lo = [jnp.where(lo, vall, zero).astype(BF16), jnp.where(lo, vswap, zero).astype(BF16)]
    v_hi = [jnp.where(lo, zero, vswap).astype(BF16), jnp.where(lo, zero, vall).astype(BF16)]
    zb = jnp.zeros((2 * BLOCK, LANES), BF16)
    gq = gq_ref[...]
    prev_valid, _, _ = _prev_key_valid()

    heads_per_kv = SWA_Q_HEADS // SWA_KV_HEADS

    q_slab = [[None] * n_groups for _ in range(n_blocks)]
    gate_slab = [[None] * n_groups for _ in range(n_blocks)]
    acc = [None] * n_blocks

    def project(blk, grp):
        hb = h[blk * BLOCK:(blk + 1) * BLOCK]
        q_slab[blk][grp] = _dot(hb, wq_ref[grp])
        gate_slab[blk][grp] = _dot(hb, wg_ref[grp])

    def scores(blk, grp):
        band = slice(blk * BLOCK, (blk + 2) * BLOCK)
        kv = (4 * grp) // heads_per_kv
        tiles = []
        for pair in range(2):
            qp = q_slab[blk][grp][:, pair * LANES:(pair + 1) * LANES]
            sq = qp * qp
            ms_lo = jnp.sum(jnp.where(lo, sq, 0.0), axis=-1, keepdims=True) / SWA_HEAD_DIM
            ms_hi = jnp.sum(jnp.where(lo, 0.0, sq), axis=-1, keepdims=True) / SWA_HEAD_DIM
            r = jnp.where(lo, lax.rsqrt(ms_lo + EPS), lax.rsqrt(ms_hi + EPS))
            qn = qp * r * gq
            qstack = jnp.concatenate([jnp.where(lo, qn, 0.0), jnp.where(lo, 0.0, qn)],
                                     axis=0).astype(BF16)
            tiles.append(_dot_nt(kdup[kv][band], qstack))
        return tiles

    def softmax(blk, grp, tiles):
        table = first if blk == 0 else 0
        probs = []
        for pair in range(2):
            for pos in range(2):
                hd = 4 * grp + 2 * pair + pos
                sink = sinks_ref[hd]
                cols = slice(pos * BLOCK, (pos + 1) * BLOCK)
                st = tiles[pair]
                s = (jnp.where(prev_valid, st[:BLOCK, cols], st[BLOCK:, cols])
                     + bias_ref[table, hd])
                m = jnp.maximum(jnp.max(s, axis=0, keepdims=True), sink)
                e = jnp.exp(s - m)
                den = jnp.sum(e, axis=0, keepdims=True) + jnp.exp(sink - m)
                pn = e * (1.0 / den)
                probs.append(jnp.where(prev_valid, pn, 0.0).astype(BF16))
                probs.append(jnp.where(prev_valid, 0.0, pn).astype(BF16))
        return jnp.concatenate(probs, axis=0)

    def finish(blk, grp, pt):
        band = slice(blk * BLOCK, (blk + 2) * BLOCK)
        kv = (4 * grp) // heads_per_kv
        va, vb = v_lo[kv][band], v_hi[kv][band]
        vbig = jnp.concatenate([jnp.concatenate([va, zb], axis=1),
                                jnp.concatenate([vb, zb], axis=1),
                                jnp.concatenate([zb, va], axis=1),
                                jnp.concatenate([zb, vb], axis=1)], axis=0)
        o = _dot_tn(pt, vbig)
        gate = gate_slab[blk][grp]
        y = (o * (gate * jax.nn.sigmoid(gate))).astype(BF16)
        part = _dot_slabs(y, wout_ref.at[grp])
        acc[blk] = part if acc[blk] is None else acc[blk] + part
        if grp == n_groups - 1:
            rows = slice(blk * BLOCK, (blk + 1) * BLOCK)
            o_ref[0, rows, :] = x[rows] + acc[blk]

    for grp in range(n_groups):
        project(0, grp)
    pending = None
    for blk in range(n_blocks):
        for grp in range(n_groups):
            tiles = scores(blk, grp)
            if blk + 1 < n_blocks:
                project(blk + 1, grp)
            if pending is not None:
                finish(*pending)
            pending = (blk, grp, softmax(blk, grp, tiles))
    finish(*pending)


def _swa_layer(x, g, w_in, g_q, sinks, w_out, k, v, bias):
    b, s, d = x.shape
    t = SWA_TILE
    wq = _slabs(w_in[:, :SWA_Q_WIDTH])
    wg = _slabs(w_in[:, SWA_Q_WIDTH:])
    wo = jnp.stack([_slabs(w) for w in w_out.reshape(SWA_Q_HEADS // 4, 4 * SWA_HEAD_DIM, d)])
    cur =pl.BlockSpec((1, t, KV_WIDTH), lambda i, j: (i, j, 0))
    prev = pl.BlockSpec((1, BLOCK, KV_WIDTH),
                        lambda i, j: (i, jnp.maximum(j * (t // BLOCK) - 1, 0), 0))
    return pl.pallas_call(
        _swa_kernel,
        grid=(b, s // t),
        in_specs=[
            pl.BlockSpec(memory_space=pltpu.SMEM),
            pl.BlockSpec((1, t, d), lambda i, j: (i, j, 0)),
            _resident((1, d)),
            _resident(wq.shape), _resident(wg.shape),
            _resident((1, LANES)),
            _resident(wo.shape),
            cur, prev, cur, prev,
            _resident(bias.shape),
        ],
        out_specs=pl.BlockSpec((1, t, d), lambda i, j: (i, j, 0)),
        out_shape=jax.ShapeDtypeStruct(x.shape, x.dtype),
        compiler_params=pltpu.CompilerParams(
            dimension_semantics=("parallel", "parallel"),
            vmem_limit_bytes=VMEM_LIMIT_BYTES),
        name="swa_layer",
    )(sinks, x, g.reshape(1, d), wq, wg,
      jnp.tile(g_q, 2).reshape(1, LANES) * (SWA_HEAD_DIM ** -0.5),
      wo, k, k, v, v, bias)


def kernel(x, a_norm_g, a_w_in, a_w_out, kv_norm_g, w_kv, k_norm_g, rel_bias,
           b_norm_g, b_w_in, b_q_norm_g, b_sinks, b_w_out):
    s = x.shape[1]
    tables = _retention_tables(s)
    for layer in range(a_w_in.shape[0]):
        x = _retention_layer(x, a_norm_g[layer], a_w_in[layer], a_w_out[layer], *tables)
    k, v = _shared_kv(x, kv_norm_g, w_kv, k_norm_g)
    bias = _bias_tables(rel_bias)
    for j in range(b_w_in.shape[0]):
        x = _swa_layer(x, b_norm_g[j], b_w_in[j], b_q_norm_g[j], b_sinks[j], b_w_out[j],
                       k, v, bias)
    return x
```

```python
import functools
import math

import jax
import jax.numpy as jnp
from jax import lax
from jax.experimental import pallas as pl
from jax.experimental.pallas import tpu as pltpu

D_MODEL = 1024
RET_HEADS = 4
RET_QK_DIM = D_MODEL // RET_HEADS
RET_V_DIM = 2 * RET_QK_DIM
RET_V_TOTAL = RET_HEADS * RET_V_DIM
RET_CHUNK = 128
SWA_HEAD_DIM = 64
SWA_Q_HEADS = D_MODEL // SWA_HEAD_DIM
SWA_KV_HEADS = SWA_Q_HEADS // 8
SWA_Q_WIDTH = SWA_Q_HEADS * SWA_HEAD_DIM
KV_WIDTH = SWA_KV_HEADS * SWA_HEAD_DIM
WINDOW = 128
BLOCK = 128
REL_BUCKETS = 32
REL_MAX_DIST = 128
EPS = 1e-6

LANES = 128
VMEM_LIMIT_BYTES = 56 * 1024 * 1024

RET_TILE = 512
KV_TILE = 512
SWA_TILE = 512

BF16 = jnp.bfloat16
F32 = jnp.float32


def _dot(a, b):
    return jnp.dot(a, b, preferred_element_type=F32)


def _dot_nt(a, b):
    return lax.dot_general(a, b, (((1,), (1,)), ((), ())), preferred_element_type=F32)


def _dot_tn(a, b):
    return lax.dot_general(a, b, (((0,), (0,)), ((), ())), preferred_element_type=F32)


SLAB = 256


def _slabs(w):
    k, n = w.shape
    return w.reshape(k, n // SLAB, SLAB).transpose(1, 0, 2).astype(BF16)


def _dot_slabs(a, w_ref):
    return jnp.concatenate([_dot(a, w_ref[c]) for c in range(w_ref.shape[0])], axis=1)


def _rms(x):
    return x * lax.rsqrt(jnp.mean(x * x, axis=-1, keepdims=True) + EPS)


def _resident(shape):
    nd = len(shape)
    return pl.BlockSpec(shape, lambda *_: (0,) * nd, pipeline_mode=pl.Buffered(1))


def _ret_kernel(x_ref, g_ref, wq_ref, wk_ref, wv_ref, wg_ref, wout_ref, cos_ref, sin_ref,
                idec_ref, qdec_ref, kdec_ref, cdec_ref, o_ref, state_ref):
    @pl.when(pl.program_id(1) == 0)
    def _():
        state_ref[...] = jnp.zeros_like(state_ref)

    x = x_ref[0]
    h = (_rms(x) * g_ref[...]).astype(BF16)
    cos = cos_ref[...]
    sin = sin_ref[...]
    half = RET_QK_DIM // 2
    n_chunks = x.shape[0] // RET_CHUNK

    def rotate(t):
        t1, t2 = t[:, :half], t[:, half:]
        return jnp.concatenate([t1 * cos - t2 * sin, t2 * cos + t1 * sin], axis=1)

    proj = [dict() for _ in range(RET_HEADS)]

    def project_q(hd):
        proj[hd]["q"] = rotate(_dot(h, wq_ref[hd])).astype(BF16)

    def project_k(hd):
        k = rotate(_dot(h, wk_ref[hd])) * (RET_QK_DIM ** -0.5)
        proj[hd]["k"] = k.astype(BF16)
        kdec = jnp.concatenate([kdec_ref[hd]] * n_chunks, axis=0)
        proj[hd]["kd"] = (k * kdec).astype(BF16)

    def project_v(hd):
        proj[hd]["v"] = _dot(h, wv_ref[hd]).astype(BF16)

    def project_gate(hd):
        proj[hd]["gate"] = _dot(h, wg_ref[hd])

    def finish(hd, outs, acc):
        o = _rms(jnp.concatenate(outs, axis=0))
        gate = proj[hd]["gate"]
        y = (o * (gate * jax.nn.sigmoid(gate))).astype(BF16)
        return acc + _dot_slabs(y, wout_ref.at[hd])

    project_q(0)
    project_k(0)
    project_v(0)
    acc = x
    pending = None
    for hd in range(RET_HEADS):
        fillers = [functools.partial(project_gate, hd)]
        if hd + 1 < RET_HEADS:
            fillers += [functools.partial(f, hd + 1) for f in (project_q, project_k, project_v)]
        qb, kb, kd, v = proj[hd]["q"], proj[hd]["k"], proj[hd]["kd"], proj[hd]["v"]
        st = state_ref[hd]
        outs = []
        for c in range(n_chunks):
            rows = slice(c * RET_CHUNK, (c + 1) * RET_CHUNK)
            qc, kc, vc = qb[rows], kb[rows], v[rows]
            scores = _dot_nt(qc, kc) * idec_ref[hd]
            inter = _dot(qc, st.astype(BF16)) * qdec_ref[hd]
            st = st * cdec_ref[hd] + _dot_tn(kd[rows], vc)
            lo_f = (c * len(fillers)) // n_chunks
            hi_f = ((c + 1) * len(fillers)) // n_chunks
            for f in fillers[lo_f:hi_f]:
                f()
            if c == 0 and pending is not None:
                acc = finish(*pending, acc)
            outs.append(_dot(scores.astype(BF16), vc) + inter)
        state_ref[hd] = st
        pending = (hd, outs)
    o_ref[0] = finish(*pending, acc)


def _retention_layer(x, g, w_in, w_out, cos, sin, idec, qdec, kdec, cdec):
    b, s, d = x.shape
    perm = jnp.concatenate([jnp.arange(0, RET_QK_DIM, 2), jnp.arange(1, RET_QK_DIM, 2)])

    def heads(w, width):
        return w.reshape(d, RET_HEADS, width).transpose(1, 0, 2)

    wq = heads(w_in[:, :D_MODEL], RET_QK_DIM)[:, :, perm].astype(BF16)
    wk = heads(w_in[:, D_MODEL:2 * D_MODEL], RET_QK_DIM)[:, :, perm].astype(BF16)
    wv = heads(w_in[:, 2 * D_MODEL:2 * D_MODEL + RET_V_TOTAL], RET_V_DIM).astype(BF16)
    wg = heads(w_in[:, 2 * D_MODEL + RET_V_TOTAL:], RET_V_DIM).astype(BF16)
    wo = jnp.stack([_slabs(w) for w in w_out.reshape(RET_HEADS, RET_V_DIM, d)])

    t = RET_TILE
    half = RET_QK_DIM // 2
    return pl.pallas_call(
        _ret_kernel,
        grid=(b, s // t),
        in_specs=[
            pl.BlockSpec((1, t, d), lambda i, j: (i, j, 0)),
            _resident((1, d)),
            _resident(wq.shape), _resident(wk.shape), _resident(wv.shape),
            _resident(wg.shape), _resident(wo.shape),
            pl.BlockSpec((t, half), lambda i, j: (j, 0)),
            pl.BlockSpec((t, half), lambda i, j: (j, 0)),
            _resident(idec.shape), _resident(qdec.shape), _resident(kdec.shape),
            _resident(cdec.shape),
        ],
        out_specs=pl.BlockSpec((1, t, d), lambda i, j: (i, j, 0)),
        out_shape=jax.ShapeDtypeStruct(x.shape, x.dtype),
        scratch_shapes=[pltpu.VMEM((RET_HEADS, RET_QK_DIM, RET_V_DIM), F32)],
        compiler_params=pltpu.CompilerParams(
            dimension_semantics=("parallel", "arbitrary"),
            vmem_limit_bytes=VMEM_LIMIT_BYTES),
        name="retention_layer",
    )(x, g.reshape(1, d), wq, wk, wv, wg, wo, cos, sin, idec, qdec, kdec, cdec)


def _retention_tables(s):
    dk = RET_QK_DIM
    angle = 1.0 / (10000.0 ** jnp.linspace(0.0, 1.0, dk // 2, dtype=F32))
    pos = jnp.arange(s, dtype=F32)[:, None]
    sin, cos = jnp.sin(pos * angle), jnp.cos(pos * angle)
    c = RET_CHUNK
    log_gamma = jnp.log(1.0 - 2.0 ** (-5.0 - jnp.arange(RET_HEADS, dtype=F32)))
    idx = jnp.arange(c, dtype=F32)
    diff = idx[:, None] - idx[None, :]
    idec = jnp.where(diff[None] >= 0,
                     jnp.exp(jnp.maximum(diff, 0.0)[None] * log_gamma[:, None, None]), 0.0)
    qdec = jnp.exp((idx + 1.0)[None, :, None] * log_gamma[:, None, None])
    kdec = jnp.exp((c - 1.0 - idx)[None, :, None] * log_gamma[:, None, None])
    cdec = jnp.exp(c * log_gamma)[:, None, None]
    return cos, sin, idec.astype(F32), qdec, kdec, cdec


def _kv_kernel(x_ref, g_ref, w_ref, gk_ref, k_ref, v_ref):
    h = (_rms(x_ref[0]) * g_ref[...]).astype(BF16)
    kv = _dot(h, w_ref[...])
    k, v = kv[:, :KV_WIDTH], kv[:, KV_WIDTH:]
    lo = lax.broadcasted_iota(jnp.int32, (1, KV_WIDTH), 1) < SWA_HEAD_DIM
    sq = k * k
    ms_lo = jnp.sum(jnp.where(lo, sq, 0.0), axis=-1, keepdims=True) / SWA_HEAD_DIM
    ms_hi = jnp.sum(jnp.where(lo, 0.0, sq), axis=-1, keepdims=True) / SWA_HEAD_DIM
    r = jnp.where(lo, lax.rsqrt(ms_lo + EPS), lax.rsqrt(ms_hi + EPS))
    k_ref[0] = k * r * gk_ref[...]
    v_ref[0] = v


def _shared_kv(x, g, w_kv, g_k):
    b, s, d = x.shape
    t = KV_TILE
    out = jax.ShapeDtypeStruct((b, s, KV_WIDTH), F32)
    return pl.pallas_call(
        _kv_kernel,
        grid=(b, s // t),
        in_specs=[
            pl.BlockSpec((1, t, d), lambda i, j: (i, j, 0)),
            _resident((1, d)),
            _resident(w_kv.shape),
            _resident((1, KV_WIDTH)),
        ],
        out_specs=[pl.BlockSpec((1, t, KV_WIDTH), lambda i, j: (i, j, 0))] * 2,
        out_shape=[out, out],
        compiler_params=pltpu.CompilerParams(
            dimension_semantics=("parallel", "parallel"),
            vmem_limit_bytes=VMEM_LIMIT_BYTES),
        name="shared_kv",
    )(x, g.reshape(1, d), w_kv.astype(BF16), jnp.tile(g_k, SWA_KV_HEADS).reshape(1, KV_WIDTH))


def _prev_key_valid():
    j = lax.broadcasted_iota(jnp.int32, (BLOCK, BLOCK), 0)
    i = lax.broadcasted_iota(jnp.int32, (BLOCK, BLOCK), 1)
    return j > i, j, i


def _bias_kernel(rel_ref, o_ref):
    hd = pl.program_id(0)
    prev, j, i = _prev_key_valid()
    dist = jnp.where(prev, i + BLOCK - j, i - j)
    max_exact = REL_BUCKETS // 2
    dist_f = jnp.maximum(dist, 1).astype(F32)
    large = max_exact + (jnp.log(dist_f / max_exact) / math.log(REL_MAX_DIST / max_exact)
                         * (REL_BUCKETS - max_exact)).astype(jnp.int32)
    large = jnp.minimum(large, REL_BUCKETS - 1)
    bucket = jnp.where(dist < max_exact, dist, large)
    bias = jnp.zeros((BLOCK, BLOCK), F32)
    for bk in range(REL_BUCKETS):
        bias = jnp.where(bucket == bk, rel_ref[bk, hd], bias)
    o_ref[0, 0] = bias
    o_ref[1, 0] = jnp.where(prev, jnp.float32(-jnp.inf), bias)


def _bias_tables(rel_bias):
    return pl.pallas_call(
        _bias_kernel,
        grid=(SWA_Q_HEADS,),
        in_specs=[pl.BlockSpec(memory_space=pltpu.SMEM)],
        out_specs=pl.BlockSpec((2, 1, BLOCK, BLOCK), lambda h: (0, h, 0, 0)),
        out_shape=jax.ShapeDtypeStruct((2, SWA_Q_HEADS, BLOCK, BLOCK), F32),
        name="rel_bias_tables",
    )(rel_bias)


def _swa_kernel(sinks_ref, x_ref, g_ref, wq_ref, wg_ref, gq_ref, wout_ref,
                kc_ref, kp_ref, vc_ref, vp_ref, bias_ref, o_ref):
    first = (pl.program_id(1) == 0).astype(jnp.int32)
    x = x_ref[0]
    n_blocks = x.shape[0] // BLOCK
    n_groups = SWA_Q_HEADS // 4
    h = (_rms(x) * g_ref[...]).astype(BF16)

    lo = lax.broadcasted_iota(jnp.int32, (1, LANES), 1) < SWA_HEAD_DIM

    kall = jnp.concatenate([kp_ref[0], kc_ref[0]], axis=0)
    kswap = pltpu.roll(kall, SWA_HEAD_DIM, 1)
    kdup = [jnp.where(lo, kall, kswap).astype(BF16), jnp.where(lo, kswap, kall).astype(BF16)]
    vall = jnp.concatenate([vp_ref[0], vc_ref[0]], axis=0)
    vswap = pltpu.roll(vall, SWA_HEAD_DIM, 1)
    zero = jnp.zeros_like(vall)
    v_lo = [jnp.where(lo, vall, zero).astype(BF16), jnp.where(lo, vswap, zero).astype(BF16)]
    v_hi = [jnp.where(lo, zero, vswap).astype(BF16), jnp.where(lo, zero, vall).astype(BF16)]
    zb = jnp.zeros((2 * BLOCK, LANES), BF16)
    gq = gq_ref[...]
    prev_valid, _, _ = _prev_key_valid()

    heads_per_kv = SWA_Q_HEADS // SWA_KV_HEADS

    q_slab = [[None] * n_groups for _ in range(n_blocks)]
    gate_slab = [[None] * n_groups for _ in range(n_blocks)]
    acc = [None] * n_blocks

    def project(blk, grp):
        hb = h[blk * BLOCK:(blk + 1) * BLOCK]
        q_slab[blk][grp] = _dot(hb, wq_ref[grp])
        gate_slab[blk][grp] = _dot(hb, wg_ref[grp])

    def scores(blk, grp):
        band = slice(blk * BLOCK, (blk + 2) * BLOCK)
        kv = (4 * grp) // heads_per_kv
        tiles = []
        for pair in range(2):
            qp = q_slab[blk][grp][:, pair * LANES:(pair + 1) * LANES]
            sq = qp * qp
            ms_lo = jnp.sum(jnp.where(lo, sq, 0.0), axis=-1, keepdims=True) / SWA_HEAD_DIM
            ms_hi = jnp.sum(jnp.where(lo, 0.0, sq), axis=-1, keepdims=True) / SWA_HEAD_DIM
            r = jnp.where(lo, lax.rsqrt(ms_lo + EPS), lax.rsqrt(ms_hi + EPS))
            qn = qp * r * gq
            qstack = jnp.concatenate([jnp.where(lo, qn, 0.0), jnp.where(lo, 0.0, qn)],
                                     axis=0).astype(BF16)
            tiles.append(_dot_nt(kdup[kv][band], qstack))
        return tiles

    def softmax(blk, grp, tiles):
        table = first if blk == 0 else 0
        probs = []
        for pair in range(2):
            for pos in range(2):
                hd = 4 * grp + 2 * pair + pos
                sink = sinks_ref[hd]
                cols = slice(pos * BLOCK, (pos + 1) * BLOCK)
                st = tiles[pair]
                s = (jnp.where(prev_valid, st[:BLOCK, cols], st[BLOCK:, cols])
                     + bias_ref[table, hd])
                m = jnp.maximum(jnp.max(s, axis=0, keepdims=True), sink)
                e = jnp.exp(s - m)
                den = jnp.sum(e, axis=0, keepdims=True) + jnp.exp(sink - m)
                pn = e * (1.0 / den)
                probs.append(jnp.where(prev_valid, pn, 0.0).astype(BF16))
                probs.append(jnp.where(prev_valid, 0.0, pn).astype(BF16))
        return jnp.concatenate(probs, axis=0)

    def finish(blk, grp, pt):
        band = slice(blk * BLOCK, (blk + 2) * BLOCK)
        kv = (4 * grp) // heads_per_kv
        va, vb = v_lo[kv][band], v_hi[kv][band]
        vbig = jnp.concatenate([jnp.concatenate([va, zb], axis=1),
                                jnp.concatenate([vb, zb], axis=1),
                                jnp.concatenate([zb, va], axis=1),
                                jnp.concatenate([zb, vb], axis=1)], axis=0)
        o = _dot_tn(pt, vbig)
        gate = gate_slab[blk][grp]
        y = (o * (gate * jax.nn.sigmoid(gate))).astype(BF16)
        part = _dot_slabs(y, wout_ref.at[grp])
        acc[blk] = part if acc[blk] is None else acc[blk] + part
        if grp == n_groups - 1:
            rows = slice(blk * BLOCK, (blk + 1) * BLOCK)
            o_ref[0, rows, :] = x[rows] + acc[blk]

    for grp in range(n_groups):
        project(0, grp)
    pending = None
    for blk in range(n_blocks):
        for grp in range(n_groups):
            tiles = scores(blk, grp)
            if blk + 1 < n_blocks:
                project(blk + 1, grp)
            if pending is not None:
                finish(*pending)
            pending = (blk, grp, softmax(blk, grp, tiles))
    finish(*pending)


def _swa_layer(x, g, w_in, g_q, sinks, w_out, k, v, bias):
    b, s, d = x.shape
    t = SWA_TILE
    wq = _slabs(w_in[:, :SWA_Q_WIDTH])
    wg = _slabs(w_in[:, SWA_Q_WIDTH:])
    wo = jnp.stack([_slabs(w) for w in w_out.reshape(SWA_Q_HEADS // 4, 4 * SWA_HEAD_DIM, d)])
    cur =pl.BlockSpec((1, t, KV_WIDTH), lambda i, j: (i, j, 0))
    prev = pl.BlockSpec((1, BLOCK, KV_WIDTH),
                        lambda i, j: (i, jnp.maximum(j * (t // BLOCK) - 1, 0), 0))
    return pl.pallas_call(
        _swa_kernel,
        grid=(b, s // t),
        in_specs=[
            pl.BlockSpec(memory_space=pltpu.SMEM),
            pl.BlockSpec((1, t, d), lambda i, j: (i, j, 0)),
            _resident((1, d)),
            _resident(wq.shape), _resident(wg.shape),
            _resident((1, LANES)),
            _resident(wo.shape),
            cur, prev, cur, prev,
            _resident(bias.shape),
        ],
        out_specs=pl.BlockSpec((1, t, d), lambda i, j: (i, j, 0)),
        out_shape=jax.ShapeDtypeStruct(x.shape, x.dtype),
        compiler_params=pltpu.CompilerParams(
            dimension_semantics=("parallel", "parallel"),
            vmem_limit_bytes=VMEM_LIMIT_BYTES),
        name="swa_layer",
    )(sinks, x, g.reshape(1, d), wq, wg,
      jnp.tile(g_q, 2).reshape(1, LANES) * (SWA_HEAD_DIM ** -0.5),
      wo, k, k, v, v, bias)


def kernel(x, a_norm_g, a_w_in, a_w_out, kv_norm_g, w_kv, k_norm_g, rel_bias,
           b_norm_g, b_w_in, b_q_norm_g, b_sinks, b_w_out):
    s = x.shape[1]
    tables = _retention_tables(s)
    for layer in range(a_w_in.shape[0]):
        x = _retention_layer(x, a_norm_g[layer], a_w_in[layer], a_w_out[layer], *tables)
    k, v = _shared_kv(x, kv_norm_g, w_kv, k_norm_g)
    bias = _bias_tables(rel_bias)
    for j in range(b_w_in.shape[0]):
        x = _swa_layer(x, b_norm_g[j], b_w_in[j], b_q_norm_g[j], b_sinks[j], b_w_out[j],
                       k, v, bias)
    return x
```

```python
import functools
import math

import jax
import jax.numpy as jnp
from jax import lax
from jax.experimental import pallas as pl
from jax.experimental.pallas import tpu as pltpu

D_MODEL = 1024
RET_HEADS = 4
RET_QK_DIM = D_MODEL // RET_HEADS
RET_V_DIM = 2 * RET_QK_DIM
RET_V_TOTAL = RET_HEADS * RET_V_DIM
RET_CHUNK = 128
SWA_HEAD_DIM = 64
SWA_Q_HEADS = D_MODEL // SWA_HEAD_DIM
SWA_KV_HEADS = SWA_Q_HEADS // 8
SWA_Q_WIDTH = SWA_Q_HEADS * SWA_HEAD_DIM
KV_WIDTH = SWA_KV_HEADS * SWA_HEAD_DIM
WINDOW = 128
BLOCK = 128
REL_BUCKETS = 32
REL_MAX_DIST = 128
EPS = 1e-6

LANES = 128
VMEM_LIMIT_BYTES = 56 * 1024 * 1024

RET_TILE = 512
KV_TILE = 512
SWA_TILE = 512

BF16 = jnp.bfloat16
F32 = jnp.float32


def _dot(a, b):
    return jnp.dot(a, b, preferred_element_type=F32)


def _dot_nt(a, b):
    return lax.dot_general(a, b, (((1,), (1,)), ((), ())), preferred_element_type=F32)


def _dot_tn(a, b):
    return lax.dot_general(a, b, (((0,), (0,)), ((), ())), preferred_element_type=F32)


SLAB = 256


def _slabs(w):
    k, n = w.shape
    return w.reshape(k, n // SLAB, SLAB).transpose(1, 0, 2).astype(BF16)


def _dot_slabs(a, w_ref):
    return jnp.concatenate([_dot(a, w_ref[c]) for c in range(w_ref.shape[0])], axis=1)


def _rms(x):
    return x * lax.rsqrt(jnp.mean(x * x, axis=-1, keepdims=True) + EPS)


def _resident(shape):
    nd = len(shape)
    return pl.BlockSpec(shape, lambda *_: (0,) * nd, pipeline_mode=pl.Buffered(1))


def _ret_kernel(x_ref, g_ref, wq_ref, wk_ref, wv_ref, wg_ref, wout_ref, cos_ref, sin_ref,
                idec_ref, qdec_ref, kdec_ref, cdec_ref, o_ref, state_ref):
    @pl.when(pl.program_id(1) == 0)
    def _():
        state_ref[...] = jnp.zeros_like(state_ref)

    x = x_ref[0]
    h = (_rms(x) * g_ref[...]).astype(BF16)
    cos = cos_ref[...]
    sin = sin_ref[...]
    half = RET_QK_DIM // 2
    n_chunks = x.shape[0] // RET_CHUNK

    def rotate(t):
        t1, t2 = t[:, :half], t[:, half:]
        return jnp.concatenate([t1 * cos - t2 * sin, t2 * cos + t1 * sin], axis=1)

    proj = [dict() for _ in range(RET_HEADS)]

    def project_q(hd):
        proj[hd]["q"] = rotate(_dot(h, wq_ref[hd])).astype(BF16)

    def project_k(hd):
        k = rotate(_dot(h, wk_ref[hd])) * (RET_QK_DIM ** -0.5)
        proj[hd]["k"] = k.astype(BF16)
        kdec = jnp.concatenate([kdec_ref[hd]] * n_chunks, axis=0)
        proj[hd]["kd"] = (k * kdec).astype(BF16)

    def project_v(hd):
        proj[hd]["v"] = _dot(h, wv_ref[hd]).astype(BF16)

    def project_gate(hd):
        proj[hd]["gate"] = _dot(h, wg_ref[hd])

    def finish(hd, outs, acc):
        o = _rms(jnp.concatenate(outs, axis=0))
        gate = proj[hd]["gate"]
        y = (o * (gate * jax.nn.sigmoid(gate))).astype(BF16)
        return acc + _dot_slabs(y, wout_ref.at[hd])

    project_q(0)
    project_k(0)
    project_v(0)
    acc = x
    pending = None
    for hd in range(RET_HEADS):
        fillers = [functools.partial(project_gate, hd)]
        if hd + 1 < RET_HEADS:
            fillers += [functools.partial(f, hd + 1) for f in (project_q, project_k, project_v)]
        qb, kb, kd, v = proj[hd]["q"], proj[hd]["k"], proj[hd]["kd"], proj[hd]["v"]
        st = state_ref[hd]
        outs = []
        for c in range(n_chunks):
            rows = slice(c * RET_CHUNK, (c + 1) * RET_CHUNK)
            qc, kc, vc = qb[rows], kb[rows], v[rows]
            scores = _dot_nt(qc, kc) * idec_ref[hd]
            inter = _dot(qc, st.astype(BF16)) * qdec_ref[hd]
            st = st * cdec_ref[hd] + _dot_tn(kd[rows], vc)
            lo_f = (c * len(fillers)) // n_chunks
            hi_f = ((c + 1) * len(fillers)) // n_chunks
            for f in fillers[lo_f:hi_f]:
                f()
            if c == 0 and pending is not None:
                acc = finish(*pending, acc)
            outs.append(_dot(scores.astype(BF16), vc) + inter)
        state_ref[hd] = st
        pending = (hd, outs)
    o_ref[0] = finish(*pending, acc)


def _retention_layer(x, g, w_in, w_out, cos, sin, idec, qdec, kdec, cdec):
    b, s, d = x.shape
    perm = jnp.concatenate([jnp.arange(0, RET_QK_DIM, 2), jnp.arange(1, RET_QK_DIM, 2)])

    def heads(w, width):
        return w.reshape(d, RET_HEADS, width).transpose(1, 0, 2)

    wq = heads(w_in[:, :D_MODEL], RET_QK_DIM)[:, :, perm].astype(BF16)
    wk = heads(w_in[:, D_MODEL:2 * D_MODEL], RET_QK_DIM)[:, :, perm].astype(BF16)
    wv = heads(w_in[:, 2 * D_MODEL:2 * D_MODEL + RET_V_TOTAL], RET_V_DIM).astype(BF16)
    wg = heads(w_in[:, 2 * D_MODEL + RET_V_TOTAL:], RET_V_DIM).astype(BF16)
    wo = jnp.stack([_slabs(w) for w in w_out.reshape(RET_HEADS, RET_V_DIM, d)])

    t = RET_TILE
    half = RET_QK_DIM // 2
    return pl.pallas_call(
        _ret_kernel,
        grid=(b, s // t),
        in_specs=[
            pl.BlockSpec((1, t, d), lambda i, j: (i, j, 0)),
            _resident((1, d)),
            _resident(wq.shape), _resident(wk.shape), _resident(wv.shape),
            _resident(wg.shape), _resident(wo.shape),
            pl.BlockSpec((t, half), lambda i, j: (j, 0)),
            pl.BlockSpec((t, half), lambda i, j: (j, 0)),
            _resident(idec.shape), _resident(qdec.shape), _resident(kdec.shape),
            _resident(cdec.shape),
        ],
        out_specs=pl.BlockSpec((1, t, d), lambda i, j: (i, j, 0)),
        out_shape=jax.ShapeDtypeStruct(x.shape, x.dtype),
        scratch_shapes=[pltpu.VMEM((RET_HEADS, RET_QK_DIM, RET_V_DIM), F32)],
        compiler_params=pltpu.CompilerParams(
            dimension_semantics=("parallel", "arbitrary"),
            vmem_limit_bytes=VMEM_LIMIT_BYTES),
        name="retention_layer",
    )(x, g.reshape(1, d), wq, wk, wv, wg, wo, cos, sin, idec, qdec, kdec, cdec)


def _retention_tables(s):
    dk = RET_QK_DIM
    angle = 1.0 / (10000.0 ** jnp.linspace(0.0, 1.0, dk // 2, dtype=F32))
    pos = jnp.arange(s, dtype=F32)[:, None]
    sin, cos = jnp.sin(pos * angle), jnp.cos(pos * angle)
    c = RET_CHUNK
    log_gamma = jnp.log(1.0 - 2.0 ** (-5.0 - jnp.arange(RET_HEADS, dtype=F32)))
    idx = jnp.arange(c, dtype=F32)
    diff = idx[:, None] - idx[None, :]
    idec = jnp.where(diff[None] >= 0,
                     jnp.exp(jnp.maximum(diff, 0.0)[None] * log_gamma[:, None, None]), 0.0)
    qdec = jnp.exp((idx + 1.0)[None, :, None] * log_gamma[:, None, None])
    kdec = jnp.exp((c - 1.0 - idx)[None, :, None] * log_gamma[:, None, None])
    cdec = jnp.exp(c * log_gamma)[:, None, None]
    return cos, sin, idec.astype(F32), qdec, kdec, cdec


def _kv_kernel(x_ref, g_ref, w_ref, gk_ref, k_ref, vt_ref):
    h = (_rms(x_ref[0]) * g_ref[...]).astype(BF16)
    kv = _dot(h, w_ref[...])
    k, v = kv[:, :KV_WIDTH], kv[:, KV_WIDTH:]
    lo = lax.broadcasted_iota(jnp.int32, (1, KV_WIDTH), 1) < SWA_HEAD_DIM
    sq = k * k
    ms_lo = jnp.sum(jnp.where(lo, sq, 0.0), axis=-1, keepdims=True) / SWA_HEAD_DIM
    ms_hi = jnp.sum(jnp.where(lo, 0.0, sq), axis=-1, keepdims=True) / SWA_HEAD_DIM
    r = jnp.where(lo, lax.rsqrt(ms_lo + EPS), lax.rsqrt(ms_hi + EPS))
    k_ref[0] = k * r * gk_ref[...]
    vt_ref[0] = v.T


def _shared_kv(x, g, w_kv, g_k):
    b, s, d = x.shape
    t = KV_TILE
    return pl.pallas_call(
        _kv_kernel,
        grid=(b, s // t),
        in_specs=[
            pl.BlockSpec((1, t, d), lambda i, j: (i, j, 0)),
            _resident((1, d)),
            _resident(w_kv.shape),
            _resident((1, KV_WIDTH)),
        ],
        out_specs=[pl.BlockSpec((1, t, KV_WIDTH), lambda i, j: (i, j, 0)),
                   pl.BlockSpec((1, KV_WIDTH, t), lambda i, j: (i, 0, j))],
        out_shape=[jax.ShapeDtypeStruct((b, s, KV_WIDTH), F32),
                   jax.ShapeDtypeStruct((b, KV_WIDTH, s), F32)],
        compiler_params=pltpu.CompilerParams(
            dimension_semantics=("parallel", "parallel"),
            vmem_limit_bytes=VMEM_LIMIT_BYTES),
        name="shared_kv",
    )(x, g.reshape(1, d), w_kv.astype(BF16), jnp.tile(g_k, SWA_KV_HEADS).reshape(1, KV_WIDTH))


def _prev_key_valid():
    j = lax.broadcasted_iota(jnp.int32, (BLOCK, BLOCK), 0)
    i = lax.broadcasted_iota(jnp.int32, (BLOCK, BLOCK), 1)
    return j > i, j, i


def _bias_kernel(rel_ref, o_ref):
    hd = pl.program_id(0)
    prev, j, i = _prev_key_valid()
    dist = jnp.where(prev, i + BLOCK - j, i - j)
    max_exact = REL_BUCKETS // 2
    dist_f = jnp.maximum(dist, 1).astype(F32)
    large = max_exact + (jnp.log(dist_f / max_exact) / math.log(REL_MAX_DIST / max_exact)
                         * (REL_BUCKETS - max_exact)).astype(jnp.int32)
    large = jnp.minimum(large, REL_BUCKETS - 1)
    bucket = jnp.where(dist < max_exact, dist, large)
    bias = jnp.zeros((BLOCK, BLOCK), F32)
    for bk in range(REL_BUCKETS):
        bias = jnp.where(bucket == bk, rel_ref[bk, hd], bias)
    o_ref[0, 0] = bias
    o_ref[1, 0] = jnp.where(prev, jnp.float32(-jnp.inf), bias)


def _bias_tables(rel_bias):
    return pl.pallas_call(
        _bias_kernel,
        grid=(SWA_Q_HEADS,),
        in_specs=[pl.BlockSpec(memory_space=pltpu.SMEM)],
        out_specs=pl.BlockSpec((2, 1, BLOCK, BLOCK), lambda h: (0, h, 0, 0)),
        out_shape=jax.ShapeDtypeStruct((2, SWA_Q_HEADS, BLOCK, BLOCK), F32),
        name="rel_bias_tables",
    )(rel_bias)


def _swa_kernel(sinks_ref, x_ref, g_ref, wq_ref, wg_ref, gq_ref, wout_ref,
                kc_ref, kp_ref, vtc_ref, vtp_ref, bias_ref, o_ref):
    first = (pl.program_id(1) == 0).astype(jnp.int32)
    x = x_ref[0]
    n_blocks = x.shape[0] // BLOCK
    n_groups = SWA_Q_HEADS // 4
    h = (_rms(x) * g_ref[...]).astype(BF16)

    lo = lax.broadcasted_iota(jnp.int32, (1, LANES), 1) < SWA_HEAD_DIM

    kall = jnp.concatenate([kp_ref[0], kc_ref[0]], axis=0)
    kswap = pltpu.roll(kall, SWA_HEAD_DIM, 1)
    kdup = [jnp.where(lo, kall, kswap).astype(BF16), jnp.where(lo, kswap, kall).astype(BF16)]
    vt = jnp.concatenate([vtp_ref[0], vtc_ref[0]], axis=1).astype(BF16)
    gq = gq_ref[...]
    prev_valid, _, _ = _prev_key_valid()

    heads_per_kv = SWA_Q_HEADS // SWA_KV_HEADS

    q_slab = [None] * n_groups
    gate_slab = [None] * n_groups
    y_blocks = [[None] * n_blocks for _ in range(n_groups)]

    def project_q(grp):
        q_slab[grp] = _dot(h, wq_ref[grp])

    def project_gate(grp):
        gate_slab[grp] = _dot(h, wg_ref[grp])

    def scores(blk, grp):
        band = slice(blk * BLOCK, (blk + 2) * BLOCK)
        kv = (4 * grp) // heads_per_kv
        tiles = []
        for pair in range(2):
            qp = q_slab[grp][blk * BLOCK:(blk + 1) * BLOCK, pair * LANES:(pair + 1) * LANES]
            sq = qp * qp
            ms_lo = jnp.sum(jnp.where(lo, sq, 0.0), axis=-1, keepdims=True) / SWA_HEAD_DIM
            ms_hi = jnp.sum(jnp.where(lo, 0.0, sq), axis=-1, keepdims=True) / SWA_HEAD_DIM
            r = jnp.where(lo, lax.rsqrt(ms_lo + EPS), lax.rsqrt(ms_hi + EPS))
            qn = qp * r * gq
            qstack = jnp.concatenate([jnp.where(lo, qn, 0.0), jnp.where(lo, 0.0, qn)],
                                     axis=0).astype(BF16)
            tiles.append(_dot_nt(kdup[kv][band], qstack))
        return tiles

    def softmax(blk, grp, tiles):
        table = first if blk == 0 else 0
        pairs = []
        for pair in range(2):
            heads = []
            for pos in range(2):
                hd = 4 * grp + 2 * pair + pos
                sink = sinks_ref[hd]
                cols = slice(pos * BLOCK, (pos + 1) * BLOCK)
                st = tiles[pair]
                s = (jnp.where(prev_valid, st[:BLOCK, cols], st[BLOCK:, cols])
                     + bias_ref[table, hd])
                m = jnp.maximum(jnp.max(s, axis=0, keepdims=True), sink)
                e = jnp.exp(s - m)
                den = jnp.sum(e, axis=0, keepdims=True) + jnp.exp(sink - m)
                pn = e * (1.0 / den)
                heads.append(jnp.concatenate([jnp.where(prev_valid, pn, 0.0),
                                              jnp.where(prev_valid, 0.0, pn)],
                                             axis=0).astype(BF16))
            pairs.append(jnp.concatenate(heads, axis=1))
        return pairs

    def finish(blk, grp, pairs):
        band = slice(blk * BLOCK, (blk + 2) * BLOCK)
        kv = (4 * grp) // heads_per_kv
        vt_band = vt[kv * SWA_HEAD_DIM:(kv + 1) * SWA_HEAD_DIM, band]
        ot = [_dot(vt_band, pairs[pair]) for pair in range(2)]
        ot = jnp.concatenate([ot[0][:, :BLOCK], ot[0][:, BLOCK:],
                              ot[1][:, :BLOCK], ot[1][:, BLOCK:]], axis=0)
        o = ot.T
        gate = gate_slab[grp][blk * BLOCK:(blk + 1) * BLOCK]
        y_blocks[grp][blk] = (o * (gate * jax.nn.sigmoid(gate))).astype(BF16)

    def out_project(grp):
        y = jnp.concatenate(y_blocks[grp], axis=0)
        part = _dot_slabs(y, wout_ref.at[grp])
        if grp == 0:
            o_ref[0] = x + part
        else:
            o_ref[0] += part

    assert n_blocks >= 3, "filler placement below uses blocks 0..2 of every head group"
    fillers = {}
    for grp in range(n_groups):
        fillers[(grp, 0)] = [functools.partial(project_gate, grp)]
        if grp + 1 < n_groups:
            fillers[(grp, 1)] = [functools.partial(project_q, grp + 1)]
        if grp >= 1:
            fillers[(grp, 2)] = [functools.partial(out_project, grp - 1)]
    project_q(0)
    pending = None
    for grp in range(n_groups):
        for blk in range(n_blocks):
            tiles = scores(blk, grp)
            for f in fillers.get((grp, blk), []):
                f()
            if pending is not None:
                finish(*pending)
            pending = (blk, grp, softmax(blk, grp, tiles))
    finish(*pending)
    out_project(n_groups - 1)


def _swa_layer(x, g, w_in, g_q, sinks, w_out, k, vt, bias):
    b, s, d = x.shape
    t = SWA_TILE
    wq = _slabs(w_in[:, :SWA_Q_WIDTH])
    wg = _slabs(w_in[:, SWA_Q_WIDTH:])
    wo = jnp.stack([_slabs(w) for w in w_out.reshape(SWA_Q_HEADS // 4, 4 * SWA_HEAD_DIM, d)])
    def prev_block(j):
        return jnp.maximum(j * (t // BLOCK) - 1, 0)

    k_cur = pl.BlockSpec((1, t, KV_WIDTH), lambda i, j: (i, j, 0))
    k_prev = pl.BlockSpec((1, BLOCK, KV_WIDTH), lambda i, j: (i, prev_block(j), 0))
    vt_cur = pl.BlockSpec((1, KV_WIDTH, t), lambda i, j: (i, 0, j))
    vt_prev = pl.BlockSpec((1, KV_WIDTH, BLOCK), lambda i, j: (i, 0, prev_block(j)))
    return pl.pallas_call(
        _swa_kernel,
        grid=(b, s // t),
        in_specs=[
            pl.BlockSpec(memory_space=pltpu.SMEM),
            pl.BlockSpec((1, t, d), lambda i, j: (i, j, 0)),
            _resident((1, d)),
            _resident(wq.shape), _resident(wg.shape),
            _resident((1, LANES)),
            _resident(wo.shape),
            k_cur, k_prev, vt_cur, vt_prev,
            _resident(bias.shape),
        ],
        out_specs=pl.BlockSpec((1, t, d), lambda i, j: (i, j, 0)),
        out_shape=jax.ShapeDtypeStruct(x.shape, x.dtype),
        compiler_params=pltpu.CompilerParams(
            dimension_semantics=("parallel", "parallel"),
            vmem_limit_bytes=VMEM_LIMIT_BYTES),
        name="swa_layer",
    )(sinks, x, g.reshape(1, d), wq, wg,
      jnp.tile(g_q, 2).reshape(1, LANES) * (SWA_HEAD_DIM ** -0.5),
      wo, k, k, vt, vt, bias)


def kernel(x, a_norm_g, a_w_in, a_w_out, kv_norm_g, w_kv, k_norm_g, rel_bias,
           b_norm_g, b_w_in, b_q_norm_g, b_sinks, b_w_out):
    s = x.shape[1]
    tables = _retention_tables(s)
    for layer in range(a_w_in.shape[0]):
        x = _retention_layer(x, a_norm_g[layer], a_w_in[layer], a_w_out[layer], *tables)
    k, vt = _shared_kv(x, kv_norm_g, w_kv, k_norm_g)
    bias = _bias_tables(rel_bias)
    for j in range(b_w_in.shape[0]):
        x = _swa_layer(x, b_norm_g[j], b_w_in[j], b_q_norm_g[j], b_sinks[j], b_w_out[j],
                       k, vt, bias)
    return x
```

```python
import functools
import math

import jax
import jax.numpy as jnp
from jax import lax
from jax.experimental import pallas as pl
from jax.experimental.pallas import tpu as pltpu

D_MODEL = 1024
RET_HEADS = 4
RET_QK_DIM = D_MODEL // RET_HEADS
RET_V_DIM = 2 * RET_QK_DIM
RET_V_TOTAL = RET_HEADS * RET_V_DIM
RET_CHUNK = 128
A_IN_WIDTH = 2 * D_MODEL + 2 * RET_V_TOTAL
SWA_HEAD_DIM = 64
SWA_Q_HEADS = D_MODEL // SWA_HEAD_DIM
SWA_KV_HEADS = SWA_Q_HEADS // 8
SWA_Q_WIDTH = SWA_Q_HEADS * SWA_HEAD_DIM
KV_WIDTH = SWA_KV_HEADS * SWA_HEAD_DIM
WINDOW = 128
BLOCK = 128
REL_BUCKETS = 32
REL_MAX_DIST = 128
EPS = 1e-6

LANES = 128
SLAB = 256
VMEM_LIMIT_BYTES = 56 * 1024 * 1024

RET_TILE = 512
KV_TILE = 512
SWA_TILE = 512
SWA_GROUP = 4

BF16 = jnp.bfloat16
F32 = jnp.float32


def _dot(a, b):
    return jnp.dot(a, b, preferred_element_type=F32)


def _dot_nt(a, b):
    return lax.dot_general(a, b, (((1,), (1,)), ((), ())), preferred_element_type=F32)


def _dot_tn(a, b):
    return lax.dot_general(a, b, (((0,), (0,)), ((), ())), preferred_element_type=F32)


def _dot_cols(a, w_refs):
    return jnp.concatenate([_dot(a, w[...]) for w in w_refs], axis=1)


def _rms(x):
    return x * lax.rsqrt(jnp.mean(x * x, axis=-1, keepdims=True) + EPS)


def _resident(shape):
    nd = len(shape)
    return pl.BlockSpec(shape, lambda *_: (0,) * nd, pipeline_mode=pl.Buffered(1))


def _weight_block(layer, rows, row_block, col_block):
    return pl.BlockSpec((None, rows, SLAB), lambda *_: (layer, row_block, col_block),
                        pipeline_mode=pl.Buffered(1))


RET_IN_BLOCKS = A_IN_WIDTH // SLAB
RET_OUT_BLOCKS = RET_HEADS * (D_MODEL // SLAB)


def _ret_kernel(x_ref, g_ref, *refs):
    w_in = refs[:RET_IN_BLOCKS]
    w_out = refs[RET_IN_BLOCKS:RET_IN_BLOCKS + RET_OUT_BLOCKS]
    (cos_ref, sin_ref, idec_ref, qdec_ref, kdec_ref, cdec_ref,
     o_ref, state_ref) = refs[RET_IN_BLOCKS + RET_OUT_BLOCKS:]
    qk_blocks = D_MODEL // SLAB
    v_per_head = RET_V_DIM // SLAB
    out_per_head = D_MODEL // SLAB

    def wq(hd):
        return w_in[hd:hd + 1]

    def wk(hd):
        return w_in[qk_blocks + hd:qk_blocks + hd + 1]

    def wv(hd):
        start = 2 * qk_blocks + hd * v_per_head
        return w_in[start:start + v_per_head]

    def wgate(hd):
        start = 2 * qk_blocks + RET_HEADS * v_per_head + hd * v_per_head
        return w_in[start:start + v_per_head]

    @pl.when(pl.program_id(1) == 0)
    def _():
        state_ref[...] = jnp.zeros_like(state_ref)

    x = x_ref[0]
    h = (_rms(x) * g_ref[...]).astype(BF16)
    cos = cos_ref[...]
    sin = sin_ref[...]
    n_chunks = x.shape[0] // RET_CHUNK
    even = (lax.broadcasted_iota(jnp.int32, (1, LANES), 1) & 1) == 0

    def rotate(t):
        pieces = []
        for c in range(0, t.shape[1], LANES):
            tc = t[:, c:c + LANES]
            pieces.append(jnp.where(even, pltpu.roll(tc, LANES - 1, 1), pltpu.roll(tc, 1, 1)))
        return t * cos + jnp.concatenate(pieces, axis=1) * sin

    proj = [dict() for _ in range(RET_HEADS)]

    def project_q(hd):
        proj[hd]["q"] = rotate(_dot_cols(h, wq(hd))).astype(BF16)

    def project_k(hd):
        k = rotate(_dot_cols(h, wk(hd))) * (RET_QK_DIM ** -0.5)
        proj[hd]["k"] = k.astype(BF16)
        kdec = jnp.concatenate([kdec_ref[hd]] * n_chunks, axis=0)
        proj[hd]["kd"] = (k * kdec).astype(BF16)

    def project_v(hd):
        proj[hd]["v"] = _dot_cols(h, wv(hd)).astype(BF16)

    def project_gate(hd):
        proj[hd]["gate"] = _dot_cols(h, wgate(hd))

    def finish(hd, outs, acc):
        o = _rms(jnp.concatenate(outs, axis=0))
        gate = proj[hd]["gate"]
        y = (o * (gate * jax.nn.sigmoid(gate))).astype(BF16)
        return acc + _dot_cols(y, w_out[hd * out_per_head:(hd + 1) * out_per_head])

    project_q(0)
    project_k(0)
    project_v(0)
    acc = x
    pending = None
    for hd in range(RET_HEADS):
        fillers = [functools.partial(project_gate, hd)]
        if hd + 1 < RET_HEADS:
            fillers += [functools.partial(f, hd + 1) for f in (project_q, project_k, project_v)]
        qb, kb, kd, v = proj[hd]["q"], proj[hd]["k"], proj[hd]["kd"], proj[hd]["v"]
        st = state_ref[hd]
        outs = []
        for c in range(n_chunks):
            rows = slice(c * RET_CHUNK, (c + 1) * RET_CHUNK)
            qc, kc, vc = qb[rows], kb[rows], v[rows]
            scores = _dot_nt(qc, kc) * idec_ref[hd]
            inter = _dot(qc, st.astype(BF16)) * qdec_ref[hd]
            st = st * cdec_ref[hd] + _dot_tn(kd[rows], vc)
            lo_f = (c * len(fillers)) // n_chunks
            hi_f = ((c + 1) * len(fillers)) // n_chunks
            for f in fillers[lo_f:hi_f]:
                f()
            if c == 0 and pending is not None:
                acc = finish(*pending, acc)
            outs.append(_dot(scores.astype(BF16), vc) + inter)
        state_ref[hd] = st
        pending = (hd, outs)
    o_ref[0] = finish(*pending, acc)


def _retention_layer(x, g, w_in, w_out, layer, cos, sin, idec, qdec, kdec, cdec):
    b, s, d = x.shape
    t = RET_TILE
    in_blocks = [_weight_block(layer, d, 0, c) for c in range(RET_IN_BLOCKS)]
    out_blocks = [_weight_block(layer, RET_V_DIM, hd, c)
                  for hd in range(RET_HEADS) for c in range(d // SLAB)]
    return pl.pallas_call(
        _ret_kernel,
        grid=(b, s // t),
        in_specs=[
            pl.BlockSpec((1, t, d), lambda i, j: (i, j, 0)),
            _resident((1, d)),
            *in_blocks, *out_blocks,
            pl.BlockSpec((t, RET_QK_DIM), lambda i, j: (j, 0)),
            pl.BlockSpec((t, RET_QK_DIM), lambda i, j: (j, 0)),
            _resident(idec.shape), _resident(qdec.shape), _resident(kdec.shape),
            _resident(cdec.shape),
        ],
        out_specs=pl.BlockSpec((1, t, d), lambda i, j: (i, j, 0)),
        out_shape=jax.ShapeDtypeStruct(x.shape, x.dtype),
        scratch_shapes=[pltpu.VMEM((RET_HEADS, RET_QK_DIM, RET_V_DIM), F32)],
        compiler_params=pltpu.CompilerParams(
            dimension_semantics=("parallel", "arbitrary"),
            vmem_limit_bytes=VMEM_LIMIT_BYTES),
        name="retention_layer",
    )(x, g.reshape(1, d), *([w_in] * RET_IN_BLOCKS), *([w_out] * RET_OUT_BLOCKS),
      cos, sin, idec, qdec, kdec, cdec)


def _retention_tables(s):
    dk = RET_QK_DIM
    angle = 1.0 / (10000.0 ** jnp.linspace(0.0, 1.0, dk // 2, dtype=F32))
    angle = jnp.repeat(angle, 2)
    pos = jnp.arange(s, dtype=F32)[:, None]
    sign = jnp.where(jnp.arange(dk) % 2 == 0, -1.0, 1.0).astype(F32)
    sin, cos = jnp.sin(pos * angle) * sign, jnp.cos(pos * angle)
    c = RET_CHUNK
    log_gamma = jnp.log(1.0 - 2.0 ** (-5.0 - jnp.arange(RET_HEADS, dtype=F32)))
    idx = jnp.arange(c, dtype=F32)
    diff = idx[:, None] - idx[None, :]
    idec = jnp.where(diff[None] >= 0,
                     jnp.exp(jnp.maximum(diff, 0.0)[None] * log_gamma[:, None, None]), 0.0)
    qdec = jnp.exp((idx + 1.0)[None, :, None] * log_gamma[:, None, None])
    kdec = jnp.exp((c - 1.0 - idx)[None, :, None] * log_gamma[:, None, None])
    cdec = jnp.exp(c * log_gamma)[:, None, None]
    return cos, sin, idec.astype(F32), qdec, kdec, cdec


def _kv_kernel(x_ref, g_ref, w_ref, gk_ref, k_ref, vt_ref):
    h = (_rms(x_ref[0]) * g_ref[...]).astype(BF16)
    kv = _dot(h, w_ref[...])
    k, v = kv[:, :KV_WIDTH], kv[:, KV_WIDTH:]
    lo = lax.broadcasted_iota(jnp.int32, (1, KV_WIDTH), 1) < SWA_HEAD_DIM
    sq = k * k
    ms_lo = jnp.sum(jnp.where(lo, sq, 0.0), axis=-1, keepdims=True) / SWA_HEAD_DIM
    ms_hi = jnp.sum(jnp.where(lo, 0.0, sq), axis=-1, keepdims=True) / SWA_HEAD_DIM
    r = jnp.where(lo, lax.rsqrt(ms_lo + EPS), lax.rsqrt(ms_hi + EPS))
    k_ref[0] = k * r * gk_ref[...]
    vt_ref[0] = v.T


def _shared_kv(x, g, w_kv, g_k):
    b, s, d = x.shape
    t = KV_TILE
    return pl.pallas_call(
        _kv_kernel,
        grid=(b, s // t),
        in_specs=[
            pl.BlockSpec((1, t, d), lambda i, j: (i, j, 0)),
            _resident((1, d)),
            _resident(w_kv.shape),
            _resident((1, KV_WIDTH)),
        ],
        out_specs=[pl.BlockSpec((1, t, KV_WIDTH), lambda i, j: (i, j, 0)),
                   pl.BlockSpec((1, KV_WIDTH, t), lambda i, j: (i, 0, j))],
        out_shape=[jax.ShapeDtypeStruct((b, s, KV_WIDTH), F32),
                   jax.ShapeDtypeStruct((b, KV_WIDTH, s), F32)],
        compiler_params=pltpu.CompilerParams(
            dimension_semantics=("parallel", "parallel"),
            vmem_limit_bytes=VMEM_LIMIT_BYTES),
        name="shared_kv",
    )(x, g.reshape(1, d), w_kv.astype(BF16), jnp.tile(g_k, SWA_KV_HEADS).reshape(1, KV_WIDTH))


def _prev_key_valid():
    j = lax.broadcasted_iota(jnp.int32, (BLOCK, BLOCK), 0)
    i = lax.broadcasted_iota(jnp.int32, (BLOCK, BLOCK), 1)
    return j > i, j, i


def _bias_kernel(rel_ref, o_ref):
    hd = pl.program_id(0)
    prev, j, i = _prev_key_valid()
    dist = jnp.where(prev, i + BLOCK - j, i - j)
    max_exact = REL_BUCKETS // 2
    dist_f = jnp.maximum(dist, 1).astype(F32)
    large = max_exact + (jnp.log(dist_f / max_exact) / math.log(REL_MAX_DIST / max_exact)
                         * (REL_BUCKETS - max_exact)).astype(jnp.int32)
    large = jnp.minimum(large, REL_BUCKETS - 1)
    bucket = jnp.where(dist < max_exact, dist, large)
    bias = jnp.zeros((BLOCK, BLOCK), F32)
    for bk in range(REL_BUCKETS):
        bias = jnp.where(bucket == bk, rel_ref[bk, hd], bias)
    o_ref[0, 0] = bias
    o_ref[1, 0] = jnp.where(prev, jnp.float32(-jnp.inf), bias)


def _bias_tables(rel_bias):
    return pl.pallas_call(
        _bias_kernel,
        grid=(SWA_Q_HEADS,),
        in_specs=[pl.BlockSpec(memory_space=pltpu.SMEM)],
        out_specs=pl.BlockSpec((2, 1, BLOCK, BLOCK), lambda h: (0, h, 0, 0)),
        out_shape=jax.ShapeDtypeStruct((2, SWA_Q_HEADS, BLOCK, BLOCK), F32),
        name="rel_bias_tables",
    )(rel_bias)


SWA_GROUPS = SWA_Q_HEADS // SWA_GROUP
SWA_IN_BLOCKS = 2 * SWA_Q_WIDTH // SLAB
SWA_OUT_BLOCKS = SWA_GROUPS * (D_MODEL // SLAB)
assert SWA_GROUP * SWA_HEAD_DIM == SLAB


def _swa_kernel(sinks_ref, x_ref, g_ref, gq_ref, *refs):
    w_in = refs[:SWA_IN_BLOCKS]
    w_out = refs[SWA_IN_BLOCKS:SWA_IN_BLOCKS + SWA_OUT_BLOCKS]
    kc_ref, kp_ref, vtc_ref, vtp_ref, bias_ref, o_ref = refs[SWA_IN_BLOCKS + SWA_OUT_BLOCKS:]
    out_per_group = D_MODEL // SLAB

    first = (pl.program_id(1) == 0).astype(jnp.int32)
    x = x_ref[0]
    n_blocks = x.shape[0] // BLOCK
    n_groups = SWA_GROUPS
    h = (_rms(x) * g_ref[...]).astype(BF16)

    lo = lax.broadcasted_iota(jnp.int32, (1, LANES), 1) < SWA_HEAD_DIM

    kall = jnp.concatenate([kp_ref[0], kc_ref[0]], axis=0)
    kswap = pltpu.roll(kall, SWA_HEAD_DIM, 1)
    kdup = [jnp.where(lo, kall, kswap).astype(BF16), jnp.where(lo, kswap, kall).astype(BF16)]
    vt = jnp.concatenate([vtp_ref[0], vtc_ref[0]], axis=1).astype(BF16)
    gq = gq_ref[...]
    prev_valid, _, _ = _prev_key_valid()

    heads_per_kv = SWA_Q_HEADS // SWA_KV_HEADS

    q_slab = [None] * n_groups
    gate_slab = [None] * n_groups
    y_blocks = [[None] * n_blocks for _ in range(n_groups)]

    def project_q(grp):
        q_slab[grp] = _dot(h, w_in[grp][...])

    def project_gate(grp):
        gate_slab[grp] = _dot(h, w_in[n_groups + grp][...])

    def scores(blk, grp):
        band = slice(blk * BLOCK, (blk + 2) * BLOCK)
        kv = (SWA_GROUP * grp) // heads_per_kv
        tiles = []
        for pair in range(2):
            qp = q_slab[grp][blk * BLOCK:(blk + 1) * BLOCK, pair * LANES:(pair + 1) * LANES]
            sq = qp * qp
            ms_lo = jnp.sum(jnp.where(lo, sq, 0.0), axis=-1, keepdims=True) / SWA_HEAD_DIM
            ms_hi = jnp.sum(jnp.where(lo, 0.0, sq), axis=-1, keepdims=True) / SWA_HEAD_DIM
            r = jnp.where(lo, lax.rsqrt(ms_lo + EPS), lax.rsqrt(ms_hi + EPS))
            qn = qp * r * gq
            qstack = jnp.concatenate([jnp.where(lo, qn, 0.0), jnp.where(lo, 0.0, qn)],
                                     axis=0).astype(BF16)
            tiles.append(_dot_nt(kdup[kv][band], qstack))
        return tiles

    def softmax(blk, grp, tiles):
        table = first if blk == 0 else 0
        pairs = []
        for pair in range(2):
            heads = []
            for pos in range(2):
                hd = SWA_GROUP * grp + 2 * pair + pos
                sink = sinks_ref[hd]
                cols = slice(pos * BLOCK, (pos + 1) * BLOCK)
                st = tiles[pair]
                s = (jnp.where(prev_valid, st[:BLOCK, cols], st[BLOCK:, cols])
                     + bias_ref[table, hd])
                m = jnp.maximum(jnp.max(s, axis=0, keepdims=True), sink)
                e = jnp.exp(s - m)
                den = jnp.sum(e, axis=0, keepdims=True) + jnp.exp(sink - m)
                pn = e * (1.0 / den)
                heads.append(jnp.concatenate([jnp.where(prev_valid, pn, 0.0),
                                              jnp.where(prev_valid, 0.0, pn)],
                                             axis=0).astype(BF16))
            pairs.append(jnp.concatenate(heads, axis=1))
        return pairs

    def finish(blk, grp, pairs):
        band = slice(blk * BLOCK, (blk + 2) * BLOCK)
        kv = (SWA_GROUP * grp) // heads_per_kv
        vt_band = vt[kv * SWA_HEAD_DIM:(kv + 1) * SWA_HEAD_DIM, band]
        ot = [_dot(vt_band, pairs[pair]) for pair in range(2)]
        ot = jnp.concatenate([ot[0][:, :BLOCK], ot[0][:, BLOCK:],
                              ot[1][:, :BLOCK], ot[1][:, BLOCK:]], axis=0)
        o = ot.T
        gate = gate_slab[grp][blk * BLOCK:(blk + 1) * BLOCK]
        y_blocks[grp][blk] = (o * (gate * jax.nn.sigmoid(gate))).astype(BF16)

    def out_project(grp):
        y = jnp.concatenate(y_blocks[grp], axis=0)
        part = _dot_cols(y, w_out[grp * out_per_group:(grp + 1) * out_per_group])
        if grp == 0:
            o_ref[0] = x + part
        else:
            o_ref[0] += part

    assert n_blocks >= 3, "filler placement below uses blocks 0..2 of every head group"
    fillers = {}
    for grp in range(n_groups):
        fillers[(grp, 0)] = [functools.partial(project_gate, grp)]
        if grp + 1 < n_groups:
            fillers[(grp, 1)] = [functools.partial(project_q, grp + 1)]
        if grp >= 1:
            fillers[(grp, 2)] = [functools.partial(out_project, grp - 1)]
    project_q(0)
    pending = None
    for grp in range(n_groups):
        for blk in range(n_blocks):
            tiles = scores(blk, grp)
            for f in fillers.get((grp, blk), []):
                f()
            if pending is not None:
                finish(*pending)
            pending = (blk, grp, softmax(blk, grp, tiles))
    finish(*pending)
    out_project(n_groups - 1)


def _swa_layer(x, g, w_in, g_q, sinks, w_out, layer, k, vt, bias):
    b, s, d = x.shape
    t = SWA_TILE
    in_blocks = [_weight_block(layer, d, 0, c) for c in range(SWA_IN_BLOCKS)]
    out_blocks = [_weight_block(layer, SLAB, grp, c)
                  for grp in range(SWA_GROUPS) for c in range(d // SLAB)]

    def prev_block(j):
        return jnp.maximum(j * (t // BLOCK) - 1, 0)

    k_cur = pl.BlockSpec((1, t, KV_WIDTH), lambda i, j: (i, j, 0))
    k_prev = pl.BlockSpec((1, BLOCK, KV_WIDTH), lambda i, j: (i, prev_block(j), 0))
    vt_cur = pl.BlockSpec((1, KV_WIDTH, t), lambda i, j: (i, 0, j))
    vt_prev = pl.BlockSpec((1, KV_WIDTH, BLOCK), lambda i, j: (i, 0, prev_block(j)))
    return pl.pallas_call(
        _swa_kernel,
        grid=(b, s // t),
        in_specs=[
            pl.BlockSpec(memory_space=pltpu.SMEM),
            pl.BlockSpec((1, t, d), lambda i, j: (i, j, 0)),
            _resident((1, d)),
            _resident((1, LANES)),
            *in_blocks, *out_blocks,
            k_cur, k_prev, vt_cur, vt_prev,
            _resident(bias.shape),
        ],
        out_specs=pl.BlockSpec((1, t, d), lambda i, j: (i, j, 0)),
        out_shape=jax.ShapeDtypeStruct(x.shape, x.dtype),
        compiler_params=pltpu.CompilerParams(
            dimension_semantics=("parallel", "parallel"),
            vmem_limit_bytes=VMEM_LIMIT_BYTES),
        name="swa_layer",
    )(sinks, x, g.reshape(1, d),
      jnp.tile(g_q, 2).reshape(1, LANES) * (SWA_HEAD_DIM ** -0.5),
      *([w_in] * SWA_IN_BLOCKS), *([w_out] * SWA_OUT_BLOCKS), k, k, vt, vt, bias)


def kernel(x, a_norm_g, a_w_in, a_w_out, kv_norm_g, w_kv, k_norm_g, rel_bias,
           b_norm_g, b_w_in, b_q_norm_g, b_sinks, b_w_out):
    s = x.shape[1]
    tables = _retention_tables(s)
    a_w_in, a_w_out = a_w_in.astype(BF16), a_w_out.astype(BF16)
    b_w_in, b_w_out = b_w_in.astype(BF16), b_w_out.astype(BF16)
    for layer in range(a_w_in.shape[0]):
        x = _retention_layer(x, a_norm_g[layer], a_w_in, a_w_out, layer, *tables)
    k, vt = _shared_kv(x, kv_norm_g, w_kv, k_norm_g)
    bias = _bias_tables(rel_bias)
    for layer in range(b_w_in.shape[0]):
        x = _swa_layer(x, b_norm_g[layer], b_w_in, b_q_norm_g[layer], b_sinks[layer], b_w_out,
                       layer, k, vt, bias)
    return x
```

```python
import functools
import math

import jax
import jax.numpy as jnp
from jax import lax
from jax.experimental import pallas as pl
from jax.experimental.pallas import tpu as pltpu

D_MODEL = 1024
RET_HEADS = 4
RET_QK_DIM = D_MODEL // RET_HEADS
RET_V_DIM = 2 * RET_QK_DIM
RET_V_TOTAL = RET_HEADS * RET_V_DIM
RET_CHUNK = 128
A_IN_WIDTH = 2 * D_MODEL + 2 * RET_V_TOTAL
SWA_HEAD_DIM = 64
SWA_Q_HEADS = D_MODEL // SWA_HEAD_DIM
SWA_KV_HEADS = SWA_Q_HEADS // 8
SWA_Q_WIDTH = SWA_Q_HEADS * SWA_HEAD_DIM
KV_WIDTH = SWA_KV_HEADS * SWA_HEAD_DIM
WINDOW = 128
BLOCK = 128
REL_BUCKETS = 32
REL_MAX_DIST = 128
EPS = 1e-6

LANES = 128
SLAB = 256
VMEM_LIMIT_BYTES = 56 * 1024 * 1024

RET_TILE = 512
KV_TILE = 512
SWA_TILE = 512
SWA_GROUP = 4

BF16 = jnp.bfloat16
F32 = jnp.float32


def _dot(a, b):
    return jnp.dot(a, b, preferred_element_type=F32)


def _dot_nt(a, b):
    return lax.dot_general(a, b, (((1,), (1,)), ((), ())), preferred_element_type=F32)


def _dot_tn(a, b):
    return lax.dot_general(a, b, (((0,), (0,)), ((), ())), preferred_element_type=F32)


def _dot_cols(a, w_refs):
    return jnp.concatenate([_dot(a, w[...]) for w in w_refs], axis=1)


def _rms(x):
    return x * lax.rsqrt(jnp.mean(x * x, axis=-1, keepdims=True) + EPS)


def _resident(shape):
    nd = len(shape)
    return pl.BlockSpec(shape, lambda *_: (0,) * nd, pipeline_mode=pl.Buffered(1))


def _next_tile_spec(b, n_tiles, t, d):
    last = b * n_tiles - 1
    return pl.BlockSpec((1, t, d), lambda i, j: (jnp.minimum(i * n_tiles + j + 1, last), 0, 0))


def _weight_block(layer, rows, row_block, col_block):
    return pl.BlockSpec((None, rows, SLAB), lambda *_: (layer, row_block, col_block),
                        pipeline_mode=pl.Buffered(1))


RET_IN_BLOCKS = A_IN_WIDTH // SLAB
RET_OUT_BLOCKS = RET_HEADS * (D_MODEL // SLAB)


def _ret_kernel(x_ref, xnext_ref, g_ref, *refs):
    w_in = refs[:RET_IN_BLOCKS]
    w_out = refs[RET_IN_BLOCKS:RET_IN_BLOCKS + RET_OUT_BLOCKS]
    (cos_ref, sin_ref, idec_ref, qdec_ref, kdec_ref, cdec_ref,
     o_ref, state_ref, h_ref) = refs[RET_IN_BLOCKS + RET_OUT_BLOCKS:]
    qk_blocks = D_MODEL // SLAB
    v_per_head = RET_V_DIM // SLAB
    out_per_head = D_MODEL // SLAB

    def wq(hd):
        return w_in[hd:hd + 1]

    def wk(hd):
        return w_in[qk_blocks + hd:qk_blocks + hd + 1]

    def wv(hd):
        start = 2 * qk_blocks + hd * v_per_head
        return w_in[start:start + v_per_head]

    def wgate(hd):
        start = 2 * qk_blocks + RET_HEADS * v_per_head + hd * v_per_head
        return w_in[start:start + v_per_head]

    @pl.when(pl.program_id(1) == 0)
    def _():
        state_ref[...] = jnp.zeros_like(state_ref)

    def normed(ref):
        return (_rms(ref[0]) * g_ref[...]).astype(BF16)

    @pl.when((pl.program_id(0) == 0) & (pl.program_id(1) == 0))
    def _():
        h_ref[...] = normed(x_ref)

    def prefetch_h():
        h_ref[...] = normed(xnext_ref)

    x = x_ref[0]
    cos = cos_ref[...]
    sin = sin_ref[...]
    n_chunks = x.shape[0] // RET_CHUNK
    even = (lax.broadcasted_iota(jnp.int32, (1, LANES), 1) & 1) == 0

    def rotate(t):
        pieces = []
        for c in range(0, t.shape[1], LANES):
            tc = t[:, c:c + LANES]
            pieces.append(jnp.where(even, pltpu.roll(tc, LANES - 1, 1), pltpu.roll(tc, 1, 1)))
        return t * cos + jnp.concatenate(pieces, axis=1) * sin

    proj = [dict() for _ in range(RET_HEADS)]

    def project_q(hd):
        proj[hd]["q"] = rotate(_dot_cols(h_ref[...], wq(hd))).astype(BF16)

    def project_k(hd):
        k = rotate(_dot_cols(h_ref[...], wk(hd))) * (RET_QK_DIM ** -0.5)
        proj[hd]["k"] = k.astype(BF16)
        kdec = jnp.concatenate([kdec_ref[hd]] * n_chunks, axis=0)
        proj[hd]["kd"] = (k * kdec).astype(BF16)

    def project_v(hd):
        proj[hd]["v"] = _dot_cols(h_ref[...], wv(hd)).astype(BF16)

    def project_gate(hd):
        proj[hd]["gate"] = _dot_cols(h_ref[...], wgate(hd))

    def finish(hd, outs, acc):
        o = _rms(jnp.concatenate(outs, axis=0))
        gate = proj[hd]["gate"]
        y = (o * (gate * jax.nn.sigmoid(gate))).astype(BF16)
        return acc + _dot_cols(y, w_out[hd * out_per_head:(hd + 1) * out_per_head])

    project_q(0)
    project_k(0)
    project_v(0)
    acc = x
    pending = None
    for hd in range(RET_HEADS):
        fillers = [functools.partial(project_gate, hd)]
        if hd + 1 < RET_HEADS:
            fillers += [functools.partial(f, hd + 1) for f in (project_q, project_k, project_v)]
        else:
            fillers += [prefetch_h]
        qb, kb, kd, v = proj[hd]["q"], proj[hd]["k"], proj[hd]["kd"], proj[hd]["v"]
        st = state_ref[hd]
        outs = []
        for c in range(n_chunks):
            rows = slice(c * RET_CHUNK, (c + 1) * RET_CHUNK)
            qc, kc, vc = qb[rows], kb[rows], v[rows]
            scores = _dot_nt(qc, kc) * idec_ref[hd]
            inter = _dot(qc, st.astype(BF16)) * qdec_ref[hd]
            st = st * cdec_ref[hd] + _dot_tn(kd[rows], vc)
            lo_f = (c * len(fillers)) // n_chunks
            hi_f = ((c + 1) * len(fillers)) // n_chunks
            for f in fillers[lo_f:hi_f]:
                f()
            if c == 0 and pending is not None:
                acc = finish(*pending, acc)
            outs.append(_dot(scores.astype(BF16), vc) + inter)
        state_ref[hd] = st
        pending = (hd, outs)
    o_ref[0] = finish(*pending, acc)


def _retention_layer(x, g, w_in, w_out, layer, cos, sin, idec, qdec, kdec, cdec):
    b, s, d = x.shape
    t = RET_TILE
    in_blocks = [_weight_block(layer, d, 0, c) for c in range(RET_IN_BLOCKS)]
    out_blocks = [_weight_block(layer, RET_V_DIM, hd, c)
                  for hd in range(RET_HEADS) for c in range(d // SLAB)]
    n_tiles = s // t
    return pl.pallas_call(
        _ret_kernel,
        grid=(b, n_tiles),
        in_specs=[
            pl.BlockSpec((1, t, d), lambda i, j: (i, j, 0)),
            _next_tile_spec(b, n_tiles, t, d),
            _resident((1, d)),
            *in_blocks, *out_blocks,
            pl.BlockSpec((t, RET_QK_DIM), lambda i, j: (j, 0)),
            pl.BlockSpec((t, RET_QK_DIM), lambda i, j: (j, 0)),
            _resident(idec.shape), _resident(qdec.shape), _resident(kdec.shape),
            _resident(cdec.shape),
        ],
        out_specs=pl.BlockSpec((1, t, d), lambda i, j: (i, j, 0)),
        out_shape=jax.ShapeDtypeStruct(x.shape, x.dtype),
        scratch_shapes=[pltpu.VMEM((RET_HEADS, RET_QK_DIM, RET_V_DIM), F32),
                        pltpu.VMEM((t, d), BF16)],
        compiler_params=pltpu.CompilerParams(
            dimension_semantics=("arbitrary", "arbitrary"),
            vmem_limit_bytes=VMEM_LIMIT_BYTES),
        name="retention_layer",
    )(x, x.reshape(b * n_tiles, t, d), g.reshape(1, d),
      *([w_in] * RET_IN_BLOCKS), *([w_out] * RET_OUT_BLOCKS),
      cos, sin, idec, qdec, kdec, cdec)


def _retention_tables(s):
    dk = RET_QK_DIM
    angle = 1.0 / (10000.0 ** jnp.linspace(0.0, 1.0, dk // 2, dtype=F32))
    angle = jnp.repeat(angle, 2)
    pos = jnp.arange(s, dtype=F32)[:, None]
    sign = jnp.where(jnp.arange(dk) % 2 == 0, -1.0, 1.0).astype(F32)
    sin, cos = jnp.sin(pos * angle) * sign, jnp.cos(pos * angle)
    c = RET_CHUNK
    log_gamma = jnp.log(1.0 - 2.0 ** (-5.0 - jnp.arange(RET_HEADS, dtype=F32)))
    idx = jnp.arange(c, dtype=F32)
    diff = idx[:, None] - idx[None, :]
    idec = jnp.where(diff[None] >= 0,
                     jnp.exp(jnp.maximum(diff, 0.0)[None] * log_gamma[:, None, None]), 0.0)
    qdec = jnp.exp((idx + 1.0)[None, :, None] * log_gamma[:, None, None])
    kdec = jnp.exp((c - 1.0 - idx)[None, :, None] * log_gamma[:, None, None])
    cdec = jnp.exp(c * log_gamma)[:, None, None]
    return cos, sin, idec.astype(F32), qdec, kdec, cdec


def _kv_kernel(x_ref, g_ref, w_ref, gk_ref, k_ref, vt_ref):
    h = (_rms(x_ref[0]) * g_ref[...]).astype(BF16)
    kv = _dot(h, w_ref[...])
    k, v = kv[:, :KV_WIDTH], kv[:, KV_WIDTH:]
    lo = lax.broadcasted_iota(jnp.int32, (1, KV_WIDTH), 1) < SWA_HEAD_DIM
    sq = k * k
    ms_lo = jnp.sum(jnp.where(lo, sq, 0.0), axis=-1, keepdims=True) / SWA_HEAD_DIM
    ms_hi = jnp.sum(jnp.where(lo, 0.0, sq), axis=-1, keepdims=True) / SWA_HEAD_DIM
    r = jnp.where(lo, lax.rsqrt(ms_lo + EPS), lax.rsqrt(ms_hi + EPS))
    k_ref[0] = k * r * gk_ref[...]
    vt_ref[0] = v.T


def _shared_kv(x, g, w_kv, g_k):
    b, s, d = x.shape
    t = KV_TILE
    return pl.pallas_call(
        _kv_kernel,
        grid=(b, s // t),
        in_specs=[
            pl.BlockSpec((1, t, d), lambda i, j: (i, j, 0)),
            _resident((1, d)),
            _resident(w_kv.shape),
            _resident((1, KV_WIDTH)),
        ],
        out_specs=[pl.BlockSpec((1, t, KV_WIDTH), lambda i, j: (i, j, 0)),
                   pl.BlockSpec((1, KV_WIDTH, t), lambda i, j: (i, 0, j))],
        out_shape=[jax.ShapeDtypeStruct((b, s, KV_WIDTH), F32),
                   jax.ShapeDtypeStruct((b, KV_WIDTH, s), F32)],
        compiler_params=pltpu.CompilerParams(
            dimension_semantics=("parallel", "parallel"),
            vmem_limit_bytes=VMEM_LIMIT_BYTES),
        name="shared_kv",
    )(x, g.reshape(1, d), w_kv.astype(BF16), jnp.tile(g_k, SWA_KV_HEADS).reshape(1, KV_WIDTH))


def _prev_key_valid():
    j = lax.broadcasted_iota(jnp.int32, (BLOCK, BLOCK), 0)
    i = lax.broadcasted_iota(jnp.int32, (BLOCK, BLOCK), 1)
    return j > i, j, i


def _bias_kernel(rel_ref, o_ref):
    hd = pl.program_id(0)
    prev, j, i = _prev_key_valid()
    dist = jnp.where(prev, i + BLOCK - j, i - j)
    max_exact = REL_BUCKETS // 2
    dist_f = jnp.maximum(dist, 1).astype(F32)
    large = max_exact + (jnp.log(dist_f / max_exact) / math.log(REL_MAX_DIST / max_exact)
                         * (REL_BUCKETS - max_exact)).astype(jnp.int32)
    large = jnp.minimum(large, REL_BUCKETS - 1)
    bucket = jnp.where(dist < max_exact, dist, large)
    bias = jnp.zeros((BLOCK, BLOCK), F32)
    for bk in range(REL_BUCKETS):
        bias = jnp.where(bucket == bk, rel_ref[bk, hd], bias)
    o_ref[0, 0] = bias
    o_ref[1, 0] = jnp.where(prev, jnp.float32(-jnp.inf), bias)


def _bias_tables(rel_bias):
    return pl.pallas_call(
        _bias_kernel,
        grid=(SWA_Q_HEADS,),
        in_specs=[pl.BlockSpec(memory_space=pltpu.SMEM)],
        out_specs=pl.BlockSpec((2, 1, BLOCK, BLOCK), lambda h: (0, h, 0, 0)),
        out_shape=jax.ShapeDtypeStruct((2, SWA_Q_HEADS, BLOCK, BLOCK), F32),
        name="rel_bias_tables",
    )(rel_bias)


SWA_GROUPS = SWA_Q_HEADS // SWA_GROUP
SWA_IN_BLOCKS = 2 * SWA_Q_WIDTH // SLAB
SWA_OUT_BLOCKS = SWA_GROUPS * (D_MODEL // SLAB)
assert SWA_GROUP * SWA_HEAD_DIM == SLAB


def _swa_kernel(sinks_ref, x_ref, xnext_ref, g_ref, gq_ref, *refs):
    w_in = refs[:SWA_IN_BLOCKS]
    w_out = refs[SWA_IN_BLOCKS:SWA_IN_BLOCKS + SWA_OUT_BLOCKS]
    (kc_ref, kp_ref, vtc_ref, vtp_ref, bias_ref,
     o_ref, h_ref, q0_ref) = refs[SWA_IN_BLOCKS + SWA_OUT_BLOCKS:]
    out_per_group = D_MODEL // SLAB

    def normed(ref):
        return (_rms(ref[0]) * g_ref[...]).astype(BF16)

    @pl.when((pl.program_id(0) == 0) & (pl.program_id(1) == 0))
    def _():
        h_ref[...] = normed(x_ref)
        q0_ref[...] = _dot(h_ref[...], w_in[0][...])

    def prefetch_h():
        h_ref[...] = normed(xnext_ref)

    def prefetch_q0():
        q0_ref[...] = _dot(h_ref[...], w_in[0][...])

    first = (pl.program_id(1) == 0).astype(jnp.int32)
    x = x_ref[0]
    n_blocks = x.shape[0] // BLOCK
    n_groups = SWA_GROUPS

    lo = lax.broadcasted_iota(jnp.int32, (1, LANES), 1) < SWA_HEAD_DIM

    kall = jnp.concatenate([kp_ref[0], kc_ref[0]], axis=0)
    kswap = pltpu.roll(kall, SWA_HEAD_DIM, 1)
    kdup = [jnp.where(lo, kall, kswap).astype(BF16), jnp.where(lo, kswap, kall).astype(BF16)]
    vt = jnp.concatenate([vtp_ref[0], vtc_ref[0]], axis=1).astype(BF16)
    gq = gq_ref[...]
    prev_valid, _, _ = _prev_key_valid()

    heads_per_kv = SWA_Q_HEADS // SWA_KV_HEADS

    q_slab = [None] * n_groups
    gate_slab = [None] * n_groups
    y_blocks = [[None] * n_blocks for _ in range(n_groups)]

    def project_q(grp):
        q_slab[grp] = _dot(h_ref[...], w_in[grp][...])

    def project_gate(grp):
        gate_slab[grp] = _dot(h_ref[...], w_in[n_groups + grp][...])

    def scores(blk, grp):
        band = slice(blk * BLOCK, (blk + 2) * BLOCK)
        kv = (SWA_GROUP * grp) // heads_per_kv
        tiles = []
        for pair in range(2):
            qp = q_slab[grp][blk * BLOCK:(blk + 1) * BLOCK, pair * LANES:(pair + 1) * LANES]
            sq = qp * qp
            ms_lo = jnp.sum(jnp.where(lo, sq, 0.0), axis=-1, keepdims=True) / SWA_HEAD_DIM
            ms_hi = jnp.sum(jnp.where(lo, 0.0, sq), axis=-1, keepdims=True) / SWA_HEAD_DIM
            r = jnp.where(lo, lax.rsqrt(ms_lo + EPS), lax.rsqrt(ms_hi + EPS))
            qn = qp * r * gq
            qstack = jnp.concatenate([jnp.where(lo, qn, 0.0), jnp.where(lo, 0.0, qn)],
                                     axis=0).astype(BF16)
            tiles.append(_dot_nt(kdup[kv][band], qstack))
        return tiles

    def softmax(blk, grp, tiles):
        table = first if blk == 0 else 0
        pairs = []
        for pair in range(2):
            heads = []
            for pos in range(2):
                hd = SWA_GROUP * grp + 2 * pair + pos
                sink = sinks_ref[hd]
                cols = slice(pos * BLOCK, (pos + 1) * BLOCK)
                st = tiles[pair]
                s = (jnp.where(prev_valid, st[:BLOCK, cols], st[BLOCK:, cols])
                     + bias_ref[table, hd])
                m = jnp.maximum(jnp.max(s, axis=0, keepdims=True), sink)
                e = jnp.exp(s - m)
                den = jnp.sum(e, axis=0, keepdims=True) + jnp.exp(sink - m)
                pn = e * (1.0 / den)
                heads.append(jnp.concatenate([jnp.where(prev_valid, pn, 0.0),
                                              jnp.where(prev_valid, 0.0, pn)],
                                             axis=0).astype(BF16))
            pairs.append(jnp.concatenate(heads, axis=1))
        return pairs

    def finish(blk, grp, pairs):
        band = slice(blk * BLOCK, (blk + 2) * BLOCK)
        kv = (SWA_GROUP * grp) // heads_per_kv
        vt_band = vt[kv * SWA_HEAD_DIM:(kv + 1) * SWA_HEAD_DIM, band]
        ot = [_dot(vt_band, pairs[pair]) for pair in range(2)]
        ot = jnp.concatenate([ot[0][:, :BLOCK], ot[0][:, BLOCK:],
                              ot[1][:, :BLOCK], ot[1][:, BLOCK:]], axis=0)
        o = ot.T
        gate = gate_slab[grp][blk * BLOCK:(blk + 1) * BLOCK]
        y_blocks[grp][blk] = (o * (gate * jax.nn.sigmoid(gate))).astype(BF16)

    def out_project(grp):
        y = jnp.concatenate(y_blocks[grp], axis=0)
        part = _dot_cols(y, w_out[grp * out_per_group:(grp + 1) * out_per_group])
        if grp == 0:
            o_ref[0] = x + part
        else:
            o_ref[0] += part

    assert n_blocks >= 3, "filler placement below uses blocks 0..2 of every head group"
    fillers = {}
    for grp in range(n_groups):
        fillers[(grp, 0)] = [functools.partial(project_gate, grp)]
        if grp + 1 < n_groups:
            fillers[(grp, 1)] = [functools.partial(project_q, grp + 1)]
        else:
            fillers[(grp, 1)] = [prefetch_h]
        if grp >= 1:
            fillers[(grp, 2)] = [functools.partial(out_project, grp - 1)]
    fillers.setdefault((n_groups - 1, n_blocks - 1), []).append(prefetch_q0)
    q_slab[0] = q0_ref
    pending = None
    for grp in range(n_groups):
        for blk in range(n_blocks):
            tiles = scores(blk, grp)
            for f in fillers.get((grp, blk), []):
                f()
            if pending is not None:
                finish(*pending)
            pending = (blk, grp, softmax(blk, grp, tiles))
    finish(*pending)
    out_project(n_groups - 1)


def _swa_layer(x, g, w_in, g_q, sinks, w_out, layer, k, vt, bias):
    b, s, d = x.shape
    t = SWA_TILE
    in_blocks = [_weight_block(layer, d, 0, c) for c in range(SWA_IN_BLOCKS)]
    out_blocks = [_weight_block(layer, SLAB, grp, c)
                  for grp in range(SWA_GROUPS) for c in range(d // SLAB)]

    def prev_block(j):
        return jnp.maximum(j * (t // BLOCK) - 1, 0)

    k_cur = pl.BlockSpec((1, t, KV_WIDTH), lambda i, j: (i, j, 0))
    k_prev = pl.BlockSpec((1, BLOCK, KV_WIDTH), lambda i, j: (i, prev_block(j), 0))
    vt_cur = pl.BlockSpec((1, KV_WIDTH, t), lambda i, j: (i, 0, j))
    vt_prev = pl.BlockSpec((1, KV_WIDTH, BLOCK), lambda i, j: (i, 0, prev_block(j)))
    n_tiles = s // t
    return pl.pallas_call(
        _swa_kernel,
        grid=(b, n_tiles),
        in_specs=[
            pl.BlockSpec(memory_space=pltpu.SMEM),
            pl.BlockSpec((1, t, d), lambda i, j: (i, j, 0)),
            _next_tile_spec(b, n_tiles, t, d),
            _resident((1, d)),
            _resident((1, LANES)),
            *in_blocks, *out_blocks,
            k_cur, k_prev, vt_cur, vt_prev,
            _resident(bias.shape),
        ],
        out_specs=pl.BlockSpec((1, t, d), lambda i, j: (i, j, 0)),
        out_shape=jax.ShapeDtypeStruct(x.shape, x.dtype),
        scratch_shapes=[pltpu.VMEM((t, d), BF16), pltpu.VMEM((t, SLAB), F32)],
        compiler_params=pltpu.CompilerParams(
            dimension_semantics=("arbitrary", "arbitrary"),
            vmem_limit_bytes=VMEM_LIMIT_BYTES),
        name="swa_layer",
    )(sinks, x, x.reshape(b * n_tiles, t, d), g.reshape(1, d),
      jnp.tile(g_q, 2).reshape(1, LANES) * (SWA_HEAD_DIM ** -0.5),
      *([w_in] * SWA_IN_BLOCKS), *([w_out] * SWA_OUT_BLOCKS), k, k, vt, vt, bias)


def kernel(x, a_norm_g, a_w_in, a_w_out, kv_norm_g, w_kv, k_norm_g, rel_bias,
           b_norm_g, b_w_in, b_q_norm_g, b_sinks, b_w_out):
    s = x.shape[1]
    tables = _retention_tables(s)
    a_w_in, a_w_out = a_w_in.astype(BF16), a_w_out.astype(BF16)
    b_w_in, b_w_out = b_w_in.astype(BF16), b_w_out.astype(BF16)
    for layer in range(a_w_in.shape[0]):
        x = _retention_layer(x, a_norm_g[layer], a_w_in, a_w_out, layer, *tables)
    k, vt = _shared_kv(x, kv_norm_g, w_kv, k_norm_g)
    bias = _bias_tables(rel_bias)
    for layer in range(b_w_in.shape[0]):
        x = _swa_layer(x, b_norm_g[layer], b_w_in, b_q_norm_g[layer], b_sinks[layer], b_w_out,
                       layer, k, vt, bias)
    return x
```

```python
import functools
import math

import jax
import jax.numpy as jnp
from jax import lax
from jax.experimental import pallas as pl
from jax.experimental.pallas import tpu as pltpu

D_MODEL = 1024
RET_HEADS = 4
RET_QK_DIM = D_MODEL // RET_HEADS
RET_V_DIM = 2 * RET_QK_DIM
RET_V_TOTAL = RET_HEADS * RET_V_DIM
RET_CHUNK = 256
A_IN_WIDTH = 2 * D_MODEL + 2 * RET_V_TOTAL
SWA_HEAD_DIM = 64
SWA_Q_HEADS = D_MODEL // SWA_HEAD_DIM
SWA_KV_HEADS = SWA_Q_HEADS // 8
SWA_Q_WIDTH = SWA_Q_HEADS * SWA_HEAD_DIM
KV_WIDTH = SWA_KV_HEADS * SWA_HEAD_DIM
WINDOW = 128
BLOCK = 128
REL_BUCKETS = 32
REL_MAX_DIST = 128
EPS = 1e-6

LANES = 128
SLAB = 256
VMEM_LIMIT_BYTES = 56 * 1024 * 1024

RET_TILE = 512
KV_TILE = 512
SWA_TILE = 512
SWA_GROUP = 4

BF16 = jnp.bfloat16
F32 = jnp.float32


def _dot(a, b):
    return jnp.dot(a, b, preferred_element_type=F32)


def _dot_nt(a, b):
    return lax.dot_general(a, b, (((1,), (1,)), ((), ())), preferred_element_type=F32)


def _dot_tn(a, b):
    return lax.dot_general(a, b, (((0,), (0,)), ((), ())), preferred_element_type=F32)


def _dot_cols(a, w_refs):
    return jnp.concatenate([_dot(a, w[...]) for w in w_refs], axis=1)


def _rms(x):
    return x * lax.rsqrt(jnp.mean(x * x, axis=-1, keepdims=True) + EPS)


def _resident(shape):
    nd = len(shape)
    return pl.BlockSpec(shape, lambda *_: (0,) * nd, pipeline_mode=pl.Buffered(1))


def _next_tile_spec(b, n_tiles, t, d):
    last = b * n_tiles - 1
    return pl.BlockSpec((1, t, d), lambda i, j: (jnp.minimum(i * n_tiles + j + 1, last), 0, 0))


def _weight_block(layer, rows, row_block, col_block):
    return pl.BlockSpec((None, rows, SLAB), lambda *_: (layer, row_block, col_block),
                        pipeline_mode=pl.Buffered(1))


RET_IN_BLOCKS = A_IN_WIDTH // SLAB
RET_OUT_BLOCKS = RET_HEADS * (D_MODEL // SLAB)


def _ret_kernel(x_ref, xnext_ref, g_ref, *refs):
    w_in = refs[:RET_IN_BLOCKS]
    w_out = refs[RET_IN_BLOCKS:RET_IN_BLOCKS + RET_OUT_BLOCKS]
    (cos_ref, sin_ref, idec_ref, qdec_ref, kdec_ref, cdec_ref,
     o_ref, state_ref, h_ref) = refs[RET_IN_BLOCKS + RET_OUT_BLOCKS:]
    qk_blocks = D_MODEL // SLAB
    v_per_head = RET_V_DIM // SLAB
    out_per_head = D_MODEL // SLAB

    def wq(hd):
        return w_in[hd:hd + 1]

    def wk(hd):
        return w_in[qk_blocks + hd:qk_blocks + hd + 1]

    def wv(hd):
        start = 2 * qk_blocks + hd * v_per_head
        return w_in[start:start + v_per_head]

    def wgate(hd):
        start = 2 * qk_blocks + RET_HEADS * v_per_head + hd * v_per_head
        return w_in[start:start + v_per_head]

    @pl.when(pl.program_id(1) == 0)
    def _():
        state_ref[...] = jnp.zeros_like(state_ref)

    def normed(ref):
        return (_rms(ref[0]) * g_ref[...]).astype(BF16)

    @pl.when((pl.program_id(0) == 0) & (pl.program_id(1) == 0))
    def _():
        h_ref[...] = normed(x_ref)

    def prefetch_h():
        h_ref[...] = normed(xnext_ref)

    x = x_ref[0]
    cos = cos_ref[...]
    sin = sin_ref[...]
    n_chunks = x.shape[0] // RET_CHUNK
    even = (lax.broadcasted_iota(jnp.int32, (1, LANES), 1) & 1) == 0

    def rotate(t):
        pieces = []
        for c in range(0, t.shape[1], LANES):
            tc = t[:, c:c + LANES]
            pieces.append(jnp.where(even, pltpu.roll(tc, LANES - 1, 1), pltpu.roll(tc, 1, 1)))
        return t * cos + jnp.concatenate(pieces, axis=1) * sin

    proj = [dict() for _ in range(RET_HEADS)]

    def project_q(hd):
        proj[hd]["q"] = rotate(_dot_cols(h_ref[...], wq(hd))).astype(BF16)

    def project_k(hd):
        k = rotate(_dot_cols(h_ref[...], wk(hd))) * (RET_QK_DIM ** -0.5)
        proj[hd]["k"] = k.astype(BF16)
        kdec = jnp.concatenate([kdec_ref[hd]] * n_chunks, axis=0)
        proj[hd]["kd"] = (k * kdec).astype(BF16)

    def project_v(hd):
        proj[hd]["v"] = _dot_cols(h_ref[...], wv(hd)).astype(BF16)

    def project_gate(hd):
        proj[hd]["gate"] = _dot_cols(h_ref[...], wgate(hd))

    def finish(hd, outs, acc):
        o = _rms(jnp.concatenate(outs, axis=0))
        gate = proj[hd]["gate"]
        y = (o * (gate * jax.nn.sigmoid(gate))).astype(BF16)
        return acc + _dot_cols(y, w_out[hd * out_per_head:(hd + 1) * out_per_head])

    project_q(0)
    project_k(0)
    project_v(0)
    acc = x
    pending = None
    for hd in range(RET_HEADS):
        fillers = [functools.partial(project_gate, hd)]
        if hd + 1 < RET_HEADS:
            fillers += [functools.partial(f, hd + 1) for f in (project_q, project_k, project_v)]
        else:
            fillers += [prefetch_h]
        qb, kb, kd, v = proj[hd]["q"], proj[hd]["k"], proj[hd]["kd"], proj[hd]["v"]
        st = state_ref[hd]
        outs = []
        for c in range(n_chunks):
            rows = slice(c * RET_CHUNK, (c + 1) * RET_CHUNK)
            qc, kc, vc = qb[rows], kb[rows], v[rows]
            scores = _dot_nt(qc, kc) * idec_ref[hd]
            inter = _dot(qc, st.astype(BF16)) * qdec_ref[hd]
            st = st * cdec_ref[hd] + _dot_tn(kd[rows], vc)
            lo_f = (c * len(fillers)) // n_chunks
            hi_f = ((c + 1) * len(fillers)) // n_chunks
            for f in fillers[lo_f:hi_f]:
                f()
            if c == 0 and pending is not None:
                acc = finish(*pending, acc)
            outs.append(_dot(scores.astype(BF16), vc) + inter)
        state_ref[hd] = st
        pending = (hd, outs)
    o_ref[0] = finish(*pending, acc)


def _retention_layer(x, g, w_in, w_out, layer, cos, sin, idec, qdec, kdec, cdec):
    b, s, d = x.shape
    t = RET_TILE
    in_blocks = [_weight_block(layer, d, 0, c) for c in range(RET_IN_BLOCKS)]
    out_blocks = [_weight_block(layer, RET_V_DIM, hd, c)
                  for hd in range(RET_HEADS) for c in range(d // SLAB)]
    n_tiles = s // t
    return pl.pallas_call(
        _ret_kernel,
        grid=(b, n_tiles),
        in_specs=[
            pl.BlockSpec((1, t, d), lambda i, j: (i, j, 0)),
            _next_tile_spec(b, n_tiles, t, d),
            _resident((1, d)),
            *in_blocks, *out_blocks,
            pl.BlockSpec((t, RET_QK_DIM), lambda i, j: (j, 0)),
            pl.BlockSpec((t, RET_QK_DIM), lambda i, j: (j, 0)),
            _resident(idec.shape), _resident(qdec.shape), _resident(kdec.shape),
            _resident(cdec.shape),
        ],
        out_specs=pl.BlockSpec((1, t, d), lambda i, j: (i, j, 0)),
        out_shape=jax.ShapeDtypeStruct(x.shape, x.dtype),
        scratch_shapes=[pltpu.VMEM((RET_HEADS, RET_QK_DIM, RET_V_DIM), F32),
                        pltpu.VMEM((t, d), BF16)],
        compiler_params=pltpu.CompilerParams(
            dimension_semantics=("arbitrary", "arbitrary"),
            vmem_limit_bytes=VMEM_LIMIT_BYTES),
        name="retention_layer",
    )(x, x.reshape(b * n_tiles, t, d), g.reshape(1, d),
      *([w_in] * RET_IN_BLOCKS), *([w_out] * RET_OUT_BLOCKS),
      cos, sin, idec, qdec, kdec, cdec)


def _retention_tables(s):
    dk = RET_QK_DIM
    angle = 1.0 / (10000.0 ** jnp.linspace(0.0, 1.0, dk // 2, dtype=F32))
    angle = jnp.repeat(angle, 2)
    pos = jnp.arange(s, dtype=F32)[:, None]
    sign = jnp.where(jnp.arange(dk) % 2 == 0, -1.0, 1.0).astype(F32)
    sin, cos = jnp.sin(pos * angle) * sign, jnp.cos(pos * angle)
    c = RET_CHUNK
    log_gamma = jnp.log(1.0 - 2.0 ** (-5.0 - jnp.arange(RET_HEADS, dtype=F32)))
    idx = jnp.arange(c, dtype=F32)
    diff = idx[:, None] - idx[None, :]
    idec = jnp.where(diff[None] >= 0,
                     jnp.exp(jnp.maximum(diff, 0.0)[None] * log_gamma[:, None, None]), 0.0)
    qdec = jnp.exp((idx + 1.0)[None, :, None] * log_gamma[:, None, None])
    kdec = jnp.exp((c - 1.0 - idx)[None, :, None] * log_gamma[:, None, None])
    cdec = jnp.exp(c * log_gamma)[:, None, None]
    return cos, sin, idec.astype(F32), qdec, kdec, cdec


def _kv_kernel(x_ref, g_ref, w_ref, gk_ref, k_ref, vt_ref):
    h = (_rms(x_ref[0]) * g_ref[...]).astype(BF16)
    kv = _dot(h, w_ref[...])
    k, v = kv[:, :KV_WIDTH], kv[:, KV_WIDTH:]
    lo = lax.broadcasted_iota(jnp.int32, (1, KV_WIDTH), 1) < SWA_HEAD_DIM
    sq = k * k
    ms_lo = jnp.sum(jnp.where(lo, sq, 0.0), axis=-1, keepdims=True) / SWA_HEAD_DIM
    ms_hi = jnp.sum(jnp.where(lo, 0.0, sq), axis=-1, keepdims=True) / SWA_HEAD_DIM
    r = jnp.where(lo, lax.rsqrt(ms_lo + EPS), lax.rsqrt(ms_hi + EPS))
    k_ref[0] = k * r * gk_ref[...]
    vt_ref[0] = v.T


def _shared_kv(x, g, w_kv, g_k):
    b, s, d = x.shape
    t = KV_TILE
    return pl.pallas_call(
        _kv_kernel,
        grid=(b, s // t),
        in_specs=[
            pl.BlockSpec((1, t, d), lambda i, j: (i, j, 0)),
            _resident((1, d)),
            _resident(w_kv.shape),
            _resident((1, KV_WIDTH)),
        ],
        out_specs=[pl.BlockSpec((1, t, KV_WIDTH), lambda i, j: (i, j, 0)),
                   pl.BlockSpec((1, KV_WIDTH, t), lambda i, j: (i, 0, j))],
        out_shape=[jax.ShapeDtypeStruct((b, s, KV_WIDTH), F32),
                   jax.ShapeDtypeStruct((b, KV_WIDTH, s), F32)],
        compiler_params=pltpu.CompilerParams(
            dimension_semantics=("parallel", "parallel"),
            vmem_limit_bytes=VMEM_LIMIT_BYTES),
        name="shared_kv",
    )(x, g.reshape(1, d), w_kv.astype(BF16), jnp.tile(g_k, SWA_KV_HEADS).reshape(1, KV_WIDTH))


def _prev_key_valid():
    j = lax.broadcasted_iota(jnp.int32, (BLOCK, BLOCK), 0)
    i = lax.broadcasted_iota(jnp.int32, (BLOCK, BLOCK), 1)
    return j > i, j, i


def _bias_kernel(rel_ref, o_ref):
    hd = pl.program_id(0)
    prev, j, i = _prev_key_valid()
    dist = jnp.where(prev, i + BLOCK - j, i - j)
    max_exact = REL_BUCKETS // 2
    dist_f = jnp.maximum(dist, 1).astype(F32)
    large = max_exact + (jnp.log(dist_f / max_exact) / math.log(REL_MAX_DIST / max_exact)
                         * (REL_BUCKETS - max_exact)).astype(jnp.int32)
    large = jnp.minimum(large, REL_BUCKETS - 1)
    bucket = jnp.where(dist < max_exact, dist, large)
    bias = jnp.zeros((BLOCK, BLOCK), F32)
    for bk in range(REL_BUCKETS):
        bias = jnp.where(bucket == bk, rel_ref[bk, hd], bias)
    o_ref[0, 0] = bias
    o_ref[1, 0] = jnp.where(prev, jnp.float32(-jnp.inf), bias)


def _bias_tables(rel_bias):
    return pl.pallas_call(
        _bias_kernel,
        grid=(SWA_Q_HEADS,),
        in_specs=[pl.BlockSpec(memory_space=pltpu.SMEM)],
        out_specs=pl.BlockSpec((2, 1, BLOCK, BLOCK), lambda h: (0, h, 0, 0)),
        out_shape=jax.ShapeDtypeStruct((2, SWA_Q_HEADS, BLOCK, BLOCK), F32),
        name="rel_bias_tables",
    )(rel_bias)


SWA_GROUPS = SWA_Q_HEADS // SWA_GROUP
SWA_IN_BLOCKS = 2 * SWA_Q_WIDTH // SLAB
SWA_OUT_BLOCKS = SWA_GROUPS * (D_MODEL // SLAB)
assert SWA_GROUP * SWA_HEAD_DIM == SLAB


def _swa_kernel(sinks_ref, x_ref, xnext_ref, g_ref, gq_ref, *refs):
    w_in = refs[:SWA_IN_BLOCKS]
    w_out = refs[SWA_IN_BLOCKS:SWA_IN_BLOCKS + SWA_OUT_BLOCKS]
    (kc_ref, kp_ref, vtc_ref, vtp_ref, bias_ref,
     o_ref, h_ref, q0_ref) = refs[SWA_IN_BLOCKS + SWA_OUT_BLOCKS:]
    out_per_group = D_MODEL // SLAB

    def normed(ref):
        return (_rms(ref[0]) * g_ref[...]).astype(BF16)

    @pl.when((pl.program_id(0) == 0) & (pl.program_id(1) == 0))
    def _():
        h_ref[...] = normed(x_ref)
        q0_ref[...] = _dot(h_ref[...], w_in[0][...])

    def prefetch_h():
        h_ref[...] = normed(xnext_ref)

    def prefetch_q0():
        q0_ref[...] = _dot(h_ref[...], w_in[0][...])

    first = (pl.program_id(1) == 0).astype(jnp.int32)
    x = x_ref[0]
    n_blocks = x.shape[0] // BLOCK
    n_groups = SWA_GROUPS

    lo = lax.broadcasted_iota(jnp.int32, (1, LANES), 1) < SWA_HEAD_DIM

    kall = jnp.concatenate([kp_ref[0], kc_ref[0]], axis=0)
    kswap = pltpu.roll(kall, SWA_HEAD_DIM, 1)
    kdup = [jnp.where(lo, kall, kswap).astype(BF16), jnp.where(lo, kswap, kall).astype(BF16)]
    vt = jnp.concatenate([vtp_ref[0], vtc_ref[0]], axis=1).astype(BF16)
    gq = gq_ref[...]
    prev_valid, _, _ = _prev_key_valid()

    heads_per_kv = SWA_Q_HEADS // SWA_KV_HEADS

    q_slab = [None] * n_groups
    gate_slab = [None] * n_groups
    y_blocks = [[None] * n_blocks for _ in range(n_groups)]

    def project_q(grp):
        q_slab[grp] = _dot(h_ref[...], w_in[grp][...])

    def project_gate(grp):
        gate_slab[grp] = _dot(h_ref[...], w_in[n_groups + grp][...])

    def scores(blk, grp):
        band = slice(blk * BLOCK, (blk + 2) * BLOCK)
        kv = (SWA_GROUP * grp) // heads_per_kv
        tiles = []
        for pair in range(2):
            qp = q_slab[grp][blk * BLOCK:(blk + 1) * BLOCK, pair * LANES:(pair + 1) * LANES]
            sq = qp * qp
            ms_lo = jnp.sum(jnp.where(lo, sq, 0.0), axis=-1, keepdims=True) / SWA_HEAD_DIM
            ms_hi = jnp.sum(jnp.where(lo, 0.0, sq), axis=-1, keepdims=True) / SWA_HEAD_DIM
            r = jnp.where(lo, lax.rsqrt(ms_lo + EPS), lax.rsqrt(ms_hi + EPS))
            qn = qp * r * gq
            qstack = jnp.concatenate([jnp.where(lo, qn, 0.0), jnp.where(lo, 0.0, qn)],
                                     axis=0).astype(BF16)
            tiles.append(_dot_nt(kdup[kv][band], qstack))
        return tiles

    def softmax(blk, grp, tiles):
        table = first if blk == 0 else 0
        pairs = []
        for pair in range(2):
            heads = []
            for pos in range(2):
                hd = SWA_GROUP * grp + 2 * pair + pos
                sink = sinks_ref[hd]
                cols = slice(pos * BLOCK, (pos + 1) * BLOCK)
                st = tiles[pair]
                s = (jnp.where(prev_valid, st[:BLOCK, cols], st[BLOCK:, cols])
                     + bias_ref[table, hd])
                m = jnp.maximum(jnp.max(s, axis=0, keepdims=True), sink)
                e = jnp.exp(s - m)
                den = jnp.sum(e, axis=0, keepdims=True) + jnp.exp(sink - m)
                pn = e * (1.0 / den)
                heads.append(jnp.concatenate([jnp.where(prev_valid, pn, 0.0),
                                              jnp.where(prev_valid, 0.0, pn)],
                                             axis=0).astype(BF16))
            pairs.append(jnp.concatenate(heads, axis=1))
        return pairs

    def finish(blk, grp, pairs):
        band = slice(blk * BLOCK, (blk + 2) * BLOCK)
        kv = (SWA_GROUP * grp) // heads_per_kv
        vt_band = vt[kv * SWA_HEAD_DIM:(kv + 1) * SWA_HEAD_DIM, band]
        ot = [_dot(vt_band, pairs[pair]) for pair in range(2)]
        ot = jnp.concatenate([ot[0][:, :BLOCK], ot[0][:, BLOCK:],
                              ot[1][:, :BLOCK], ot[1][:, BLOCK:]], axis=0)
        o = ot.T
        gate = gate_slab[grp][blk * BLOCK:(blk + 1) * BLOCK]
        y_blocks[grp][blk] = (o * (gate * jax.nn.sigmoid(gate))).astype(BF16)

    def out_project(grp):
        y = jnp.concatenate(y_blocks[grp], axis=0)
        part = _dot_cols(y, w_out[grp * out_per_group:(grp + 1) * out_per_group])
        if grp == 0:
            o_ref[0] = x + part
        else:
            o_ref[0] += part

    items = [(blk, grp) for grp in range(n_groups) for blk in range(n_blocks)]
    last = len(items) - 1
    jobs = []
    for grp in range(n_groups):
        jobs.append((functools.partial(project_gate, grp), 0, grp * n_blocks + 1))
        if grp + 1 < n_groups:
            jobs.append((functools.partial(project_q, grp + 1), 0, (grp + 1) * n_blocks - 2))
        if grp >= 1:
            jobs.append((functools.partial(out_project, grp - 1), grp * n_blocks + 1, last))
    jobs += [(prefetch_h, 0, last), (prefetch_q0, n_blocks - 1, last)]
    spare = len(items) - len(jobs)
    stride = len(items) // (spare + 1)
    slots = [i for i in range(len(items)) if (i + 1) % stride or i >= spare * stride]
    fillers = {}
    for (job, first_item, last_item), slot in zip(jobs, slots):
        assert first_item <= slot <= last_item, "filler outside the items it may run in"
        fillers[slot] = job
    q_slab[0] = q0_ref
    pending = None
    tiles = scores(*items[0])
    for idx, (blk, grp) in enumerate(items):
        next_tiles = scores(*items[idx + 1]) if idx + 1 < len(items) else None
        if idx in fillers:
            fillers[idx]()
        if pending is not None:
            finish(*pending)
        pending = (blk, grp, softmax(blk, grp, tiles))
        tiles = next_tiles
    finish(*pending)
    out_project(n_groups - 1)


def _swa_layer(x, g, w_in, g_q, sinks, w_out, layer, k, vt, bias):
    b, s, d = x.shape
    t = SWA_TILE
    in_blocks = [_weight_block(layer, d, 0, c) for c in range(SWA_IN_BLOCKS)]
    out_blocks = [_weight_block(layer, SLAB, grp, c)
                  for grp in range(SWA_GROUPS) for c in range(d // SLAB)]

    def prev_block(j):
        return jnp.maximum(j * (t // BLOCK) - 1, 0)

    k_cur = pl.BlockSpec((1, t, KV_WIDTH), lambda i, j: (i, j, 0))
    k_prev = pl.BlockSpec((1, BLOCK, KV_WIDTH), lambda i, j: (i, prev_block(j), 0))
    vt_cur = pl.BlockSpec((1, KV_WIDTH, t), lambda i, j: (i, 0, j))
    vt_prev = pl.BlockSpec((1, KV_WIDTH, BLOCK), lambda i, j: (i, 0, prev_block(j)))
    n_tiles = s // t
    return pl.pallas_call(
        _swa_kernel,
        grid=(b, n_tiles),
        in_specs=[
            pl.BlockSpec(memory_space=pltpu.SMEM),
            pl.BlockSpec((1, t, d), lambda i, j: (i, j, 0)),
            _next_tile_spec(b, n_tiles, t, d),
            _resident((1, d)),
            _resident((1, LANES)),
            *in_blocks, *out_blocks,
            k_cur, k_prev, vt_cur, vt_prev,
            _resident(bias.shape),
        ],
        out_specs=pl.BlockSpec((1, t, d), lambda i, j: (i, j, 0)),
        out_shape=jax.ShapeDtypeStruct(x.shape, x.dtype),
        scratch_shapes=[pltpu.VMEM((t, d), BF16), pltpu.VMEM((t, SLAB), F32)],
        compiler_params=pltpu.CompilerParams(
            dimension_semantics=("arbitrary", "arbitrary"),
            vmem_limit_bytes=VMEM_LIMIT_BYTES),
        name="swa_layer",
    )(sinks, x, x.reshape(b * n_tiles, t, d), g.reshape(1, d),
      jnp.tile(g_q, 2).reshape(1, LANES) * (SWA_HEAD_DIM ** -0.5),
      *([w_in] * SWA_IN_BLOCKS), *([w_out] * SWA_OUT_BLOCKS), k, k, vt, vt, bias)


def kernel(x, a_norm_g, a_w_in, a_w_out, kv_norm_g, w_kv, k_norm_g, rel_bias,
           b_norm_g, b_w_in, b_q_norm_g, b_sinks, b_w_out):
    s = x.shape[1]
    tables = _retention_tables(s)
    a_w_in, a_w_out = a_w_in.astype(BF16), a_w_out.astype(BF16)
    b_w_in, b_w_out = b_w_in.astype(BF16), b_w_out.astype(BF16)
    for layer in range(a_w_in.shape[0]):
        x = _retention_layer(x, a_norm_g[layer], a_w_in, a_w_out, layer, *tables)
    k, vt = _shared_kv(x, kv_norm_g, w_kv, k_norm_g)
    bias = _bias_tables(rel_bias)
    for layer in range(b_w_in.shape[0]):
        x = _swa_layer(x, b_norm_g[layer], b_w_in, b_q_norm_g[layer], b_sinks[layer], b_w_out,
                       layer, k, vt, bias)
    return x
```

```python
import functools
import math

import jax
import jax.numpy as jnp
from jax import lax
from jax.experimental import pallas as pl
from jax.experimental.pallas import tpu as pltpu

D_MODEL = 1024
RET_HEADS = 4
RET_QK_DIM = D_MODEL // RET_HEADS
RET_V_DIM = 2 * RET_QK_DIM
RET_V_TOTAL = RET_HEADS * RET_V_DIM
RET_CHUNK = 128
A_IN_WIDTH = 2 * D_MODEL + 2 * RET_V_TOTAL
SWA_HEAD_DIM = 64
SWA_Q_HEADS = D_MODEL // SWA_HEAD_DIM
SWA_KV_HEADS = SWA_Q_HEADS // 8
SWA_Q_WIDTH = SWA_Q_HEADS * SWA_HEAD_DIM
KV_WIDTH = SWA_KV_HEADS * SWA_HEAD_DIM
WINDOW = 128
BLOCK = 128
REL_BUCKETS = 32
REL_MAX_DIST = 128
EPS = 1e-6

LANES = 128
SLAB = 256
VMEM_LIMIT_BYTES = 56 * 1024 * 1024

RET_TILE = 512
KV_TILE = 2048
SWA_TILE = 1024
SWA_GROUP = 4

BF16 = jnp.bfloat16
F32 = jnp.float32


def _dot(a, b):
    return jnp.dot(a, b, preferred_element_type=F32)


def _dot_nt(a, b):
    return lax.dot_general(a, b, (((1,), (1,)), ((), ())), preferred_element_type=F32)


def _dot_tn(a, b):
    return lax.dot_general(a, b, (((0,), (0,)), ((), ())), preferred_element_type=F32)


def _dot_cols(a, w_refs):
    return jnp.concatenate([_dot(a, w[...]) for w in w_refs], axis=1)


def _rms(x):
    return x * lax.rsqrt(jnp.mean(x * x, axis=-1, keepdims=True) + EPS)


def _resident(shape):
    nd = len(shape)
    return pl.BlockSpec(shape, lambda *_: (0,) * nd, pipeline_mode=pl.Buffered(1))


def _next_tile_spec(b, n_tiles, t, d):
    last = b * n_tiles - 1
    return pl.BlockSpec((1, t, d), lambda i, j: (jnp.minimum(i * n_tiles + j + 1, last), 0, 0))


def _weight_block(layer, rows, row_block, col_block):
    return pl.BlockSpec((None, rows, SLAB), lambda *_: (layer, row_block, col_block),
                        pipeline_mode=pl.Buffered(1))


RET_IN_BLOCKS = A_IN_WIDTH // SLAB
RET_OUT_BLOCKS = RET_HEADS * (D_MODEL // SLAB)


def _ret_kernel(x_ref, xnext_ref, g_ref, *refs):
    w_in = refs[:RET_IN_BLOCKS]
    w_out = refs[RET_IN_BLOCKS:RET_IN_BLOCKS + RET_OUT_BLOCKS]
    (cos_ref, sin_ref, idec_ref, qdec_ref, kdec_ref, cdec_ref,
     o_ref, state_ref, h_ref) = refs[RET_IN_BLOCKS + RET_OUT_BLOCKS:]
    qk_blocks = D_MODEL // SLAB
    v_per_head = RET_V_DIM // SLAB
    out_per_head = D_MODEL // SLAB

    def wq(hd):
        return w_in[hd:hd + 1]

    def wk(hd):
        return w_in[qk_blocks + hd:qk_blocks + hd + 1]

    def wv(hd):
        start = 2 * qk_blocks + hd * v_per_head
        return w_in[start:start + v_per_head]

    def wgate(hd):
        start = 2 * qk_blocks + RET_HEADS * v_per_head + hd * v_per_head
        return w_in[start:start + v_per_head]

    @pl.when(pl.program_id(1) == 0)
    def _():
        state_ref[...] = jnp.zeros_like(state_ref)

    def normed(ref):
        return (_rms(ref[0]) * g_ref[...]).astype(BF16)

    @pl.when((pl.program_id(0) == 0) & (pl.program_id(1) == 0))
    def _():
        h_ref[...] = normed(x_ref)

    def prefetch_h():
        h_ref[...] = normed(xnext_ref)

    x = x_ref[0]
    cos = cos_ref[...]
    sin = sin_ref[...]
    n_chunks = x.shape[0] // RET_CHUNK
    even = (lax.broadcasted_iota(jnp.int32, (1, LANES), 1) & 1) == 0

    def rotate(t):
        pieces = []
        for c in range(0, t.shape[1], LANES):
            tc = t[:, c:c + LANES]
            pieces.append(jnp.where(even, pltpu.roll(tc, LANES - 1, 1), pltpu.roll(tc, 1, 1)))
        return t * cos + jnp.concatenate(pieces, axis=1) * sin

    proj = [dict() for _ in range(RET_HEADS)]

    def project_q(hd):
        proj[hd]["q"] = rotate(_dot_cols(h_ref[...], wq(hd))).astype(BF16)

    def project_k(hd):
        k = rotate(_dot_cols(h_ref[...], wk(hd))) * (RET_QK_DIM ** -0.5)
        proj[hd]["k"] = k.astype(BF16)
        kdec = jnp.concatenate([kdec_ref[hd]] * n_chunks, axis=0)
        proj[hd]["kd"] = (k * kdec).astype(BF16)

    def project_v(hd):
        proj[hd]["v"] = _dot_cols(h_ref[...], wv(hd)).astype(BF16)

    def project_gate(hd):
        proj[hd]["gate"] = _dot_cols(h_ref[...], wgate(hd))

    def finish(hd, outs, acc):
        o = _rms(jnp.concatenate(outs, axis=0))
        gate = proj[hd]["gate"]
        y = (o * (gate * jax.nn.sigmoid(gate))).astype(BF16)
        return acc + _dot_cols(y, w_out[hd * out_per_head:(hd + 1) * out_per_head])

    project_q(0)
    project_k(0)
    project_v(0)
    acc = x
    pending = None
    for hd in range(RET_HEADS):
        fillers = [functools.partial(project_gate, hd)]
        if hd + 1 < RET_HEADS:
            fillers += [functools.partial(f, hd + 1) for f in (project_q, project_k, project_v)]
        else:
            fillers += [prefetch_h]
        qb, kb, kd, v = proj[hd]["q"], proj[hd]["k"], proj[hd]["kd"], proj[hd]["v"]
        st = state_ref[hd]
        outs = []
        for c in range(n_chunks):
            rows = slice(c * RET_CHUNK, (c + 1) * RET_CHUNK)
            qc, kc, vc = qb[rows], kb[rows], v[rows]
            scores = _dot_nt(qc, kc) * idec_ref[hd]
            inter = _dot(qc, st.astype(BF16)) * qdec_ref[hd]
            st = st * cdec_ref[hd] + _dot_tn(kd[rows], vc)
            lo_f = (c * len(fillers)) // n_chunks
            hi_f = ((c + 1) * len(fillers)) // n_chunks
            for f in fillers[lo_f:hi_f]:
                f()
            if c == 0 and pending is not None:
                acc = finish(*pending, acc)
            outs.append(_dot(scores.astype(BF16), vc) + inter)
        state_ref[hd] = st
        pending = (hd, outs)
    o_ref[0] = finish(*pending, acc)


def _retention_layer(x, g, w_in, w_out, layer, cos, sin, idec, qdec, kdec, cdec):
    b, s, d = x.shape
    t = RET_TILE
    in_blocks = [_weight_block(layer, d, 0, c) for c in range(RET_IN_BLOCKS)]
    out_blocks = [_weight_block(layer, RET_V_DIM, hd, c)
                  for hd in range(RET_HEADS) for c in range(d // SLAB)]
    n_tiles = s // t
    return pl.pallas_call(
        _ret_kernel,
        grid=(b, n_tiles),
        in_specs=[
            pl.BlockSpec((1, t, d), lambda i, j: (i, j, 0)),
            _next_tile_spec(b, n_tiles, t, d),
            _resident((1, d)),
            *in_blocks, *out_blocks,
            pl.BlockSpec((t, RET_QK_DIM), lambda i, j: (j, 0)),
            pl.BlockSpec((t, RET_QK_DIM), lambda i, j: (j, 0)),
            _resident(idec.shape), _resident(qdec.shape), _resident(kdec.shape),
            _resident(cdec.shape),
        ],
        out_specs=pl.BlockSpec((1, t, d), lambda i, j: (i, j, 0)),
        out_shape=jax.ShapeDtypeStruct(x.shape, x.dtype),
        scratch_shapes=[pltpu.VMEM((RET_HEADS, RET_QK_DIM, RET_V_DIM), F32),
                        pltpu.VMEM((t, d), BF16)],
        compiler_params=pltpu.CompilerParams(
            dimension_semantics=("arbitrary", "arbitrary"),
            vmem_limit_bytes=VMEM_LIMIT_BYTES),
        name="retention_layer",
    )(x, x.reshape(b * n_tiles, t, d), g.reshape(1, d),
      *([w_in] * RET_IN_BLOCKS), *([w_out] * RET_OUT_BLOCKS),
      cos, sin, idec, qdec, kdec, cdec)


def _retention_tables(s):
    dk = RET_QK_DIM
    angle = 1.0 / (10000.0 ** jnp.linspace(0.0, 1.0, dk // 2, dtype=F32))
    angle = jnp.repeat(angle, 2)
    pos = jnp.arange(s, dtype=F32)[:, None]
    sign = jnp.where(jnp.arange(dk) % 2 == 0, -1.0, 1.0).astype(F32)
    sin, cos = jnp.sin(pos * angle) * sign, jnp.cos(pos * angle)
    c = RET_CHUNK
    log_gamma = jnp.log(1.0 - 2.0 ** (-5.0 - jnp.arange(RET_HEADS, dtype=F32)))
    idx = jnp.arange(c, dtype=F32)
    diff = idx[:, None] - idx[None, :]
    idec = jnp.where(diff[None] >= 0,
                     jnp.exp(jnp.maximum(diff, 0.0)[None] * log_gamma[:, None, None]), 0.0)
    qdec = jnp.exp((idx + 1.0)[None, :, None] * log_gamma[:, None, None])
    kdec = jnp.exp((c - 1.0 - idx)[None, :, None] * log_gamma[:, None, None])
    cdec = jnp.exp(c * log_gamma)[:, None, None]
    return cos, sin, idec.astype(F32), qdec, kdec, cdec


def _kv_kernel(x_ref, g_ref, w_ref, gk_ref, k_ref, vt_ref):
    h = (_rms(x_ref[0]) * g_ref[...]).astype(BF16)
    kv = _dot(h, w_ref[...])
    k, v = kv[:, :KV_WIDTH], kv[:, KV_WIDTH:]
    lo = lax.broadcasted_iota(jnp.int32, (1, KV_WIDTH), 1) < SWA_HEAD_DIM
    sq = k * k
    ms_lo = jnp.sum(jnp.where(lo, sq, 0.0), axis=-1, keepdims=True) / SWA_HEAD_DIM
    ms_hi = jnp.sum(jnp.where(lo, 0.0, sq), axis=-1, keepdims=True) / SWA_HEAD_DIM
    r = jnp.where(lo, lax.rsqrt(ms_lo + EPS), lax.rsqrt(ms_hi + EPS))
    k_ref[0] = k * r * gk_ref[...]
    vt_ref[0] = v.T


def _shared_kv(x, g, w_kv, g_k):
    b, s, d = x.shape
    t = KV_TILE
    return pl.pallas_call(
        _kv_kernel,
        grid=(b, s // t),
        in_specs=[
            pl.BlockSpec((1, t, d), lambda i, j: (i, j, 0)),
            _resident((1, d)),
            _resident(w_kv.shape),
            _resident((1, KV_WIDTH)),
        ],
        out_specs=[pl.BlockSpec((1, t, KV_WIDTH), lambda i, j: (i, j, 0)),
                   pl.BlockSpec((1, KV_WIDTH, t), lambda i, j: (i, 0, j))],
        out_shape=[jax.ShapeDtypeStruct((b, s, KV_WIDTH), F32),
                   jax.ShapeDtypeStruct((b, KV_WIDTH, s), F32)],
        compiler_params=pltpu.CompilerParams(
            dimension_semantics=("parallel", "parallel"),
            vmem_limit_bytes=VMEM_LIMIT_BYTES),
        name="shared_kv",
    )(x, g.reshape(1, d), w_kv.astype(BF16), jnp.tile(g_k, SWA_KV_HEADS).reshape(1, KV_WIDTH))


def _prev_key_valid():
    j = lax.broadcasted_iota(jnp.int32, (BLOCK, BLOCK), 0)
    i = lax.broadcasted_iota(jnp.int32, (BLOCK, BLOCK), 1)
    return j > i, j, i


def _bias_kernel(rel_ref, o_ref):
    hd = pl.program_id(0)
    prev, j, i = _prev_key_valid()
    dist = jnp.where(prev, i + BLOCK - j, i - j)
    max_exact = REL_BUCKETS // 2
    dist_f = jnp.maximum(dist, 1).astype(F32)
    large = max_exact + (jnp.log(dist_f / max_exact) / math.log(REL_MAX_DIST / max_exact)
                         * (REL_BUCKETS - max_exact)).astype(jnp.int32)
    large = jnp.minimum(large, REL_BUCKETS - 1)
    bucket = jnp.where(dist < max_exact, dist, large)
    bias = jnp.zeros((BLOCK, BLOCK), F32)
    for bk in range(REL_BUCKETS):
        bias = jnp.where(bucket == bk, rel_ref[bk, hd], bias)
    o_ref[0, 0] = bias
    o_ref[1, 0] = jnp.where(prev, jnp.float32(-jnp.inf), bias)


def _bias_tables(rel_bias):
    return pl.pallas_call(
        _bias_kernel,
        grid=(SWA_Q_HEADS,),
        in_specs=[pl.BlockSpec(memory_space=pltpu.SMEM)],
        out_specs=pl.BlockSpec((2, 1, BLOCK, BLOCK), lambda h: (0, h, 0, 0)),
        out_shape=jax.ShapeDtypeStruct((2, SWA_Q_HEADS, BLOCK, BLOCK), F32),
        name="rel_bias_tables",
    )(rel_bias)


SWA_GROUPS = SWA_Q_HEADS // SWA_GROUP
SWA_IN_BLOCKS = 2 * SWA_Q_WIDTH // SLAB
SWA_OUT_BLOCKS = SWA_GROUPS * (D_MODEL // SLAB)
assert SWA_GROUP * SWA_HEAD_DIM == SLAB


def _swa_kernel(sinks_ref, x_ref, xnext_ref, g_ref, gq_ref, *refs):
    w_in = refs[:SWA_IN_BLOCKS]
    w_out = refs[SWA_IN_BLOCKS:SWA_IN_BLOCKS + SWA_OUT_BLOCKS]
    (kc_ref, kp_ref, vtc_ref, vtp_ref, bias_ref,
     o_ref, h_ref, q0_ref) = refs[SWA_IN_BLOCKS + SWA_OUT_BLOCKS:]
    out_per_group = D_MODEL // SLAB

    def normed(ref):
        return (_rms(ref[0]) * g_ref[...]).astype(BF16)

    @pl.when((pl.program_id(0) == 0) & (pl.program_id(1) == 0))
    def _():
        h_ref[...] = normed(x_ref)
        q0_ref[...] = _dot(h_ref[...], w_in[0][...])

    def prefetch_h():
        h_ref[...] = normed(xnext_ref)

    def prefetch_q0():
        q0_ref[...] = _dot(h_ref[...], w_in[0][...])

    first = (pl.program_id(1) == 0).astype(jnp.int32)
    x = x_ref[0]
    n_blocks = x.shape[0] // BLOCK
    n_groups = SWA_GROUPS

    lo = lax.broadcasted_iota(jnp.int32, (1, LANES), 1) < SWA_HEAD_DIM

    kall = jnp.concatenate([kp_ref[0], kc_ref[0]], axis=0)
    kswap = pltpu.roll(kall, SWA_HEAD_DIM, 1)
    kdup = [jnp.where(lo, kall, kswap).astype(BF16), jnp.where(lo, kswap, kall).astype(BF16)]
    vt = jnp.concatenate([vtp_ref[0], vtc_ref[0]], axis=1).astype(BF16)
    gq = gq_ref[...]
    prev_valid, _, _ = _prev_key_valid()

    heads_per_kv = SWA_Q_HEADS // SWA_KV_HEADS

    q_slab = [None] * n_groups
    gate_slab = [None] * n_groups
    y_blocks = [[None] * n_blocks for _ in range(n_groups)]

    def project_q(grp):
        q_slab[grp] = _dot(h_ref[...], w_in[grp][...])

    def project_gate(grp):
        gate_slab[grp] = _dot(h_ref[...], w_in[n_groups + grp][...])

    def scores(blk, grp):
        band = slice(blk * BLOCK, (blk + 2) * BLOCK)
        kv = (SWA_GROUP * grp) // heads_per_kv
        tiles = []
        for pair in range(2):
            qp = q_slab[grp][blk * BLOCK:(blk + 1) * BLOCK, pair * LANES:(pair + 1) * LANES]
            sq = qp * qp
            ms_lo = jnp.sum(jnp.where(lo, sq, 0.0), axis=-1, keepdims=True) / SWA_HEAD_DIM
            ms_hi = jnp.sum(jnp.where(lo, 0.0, sq), axis=-1, keepdims=True) / SWA_HEAD_DIM
            r = jnp.where(lo, lax.rsqrt(ms_lo + EPS), lax.rsqrt(ms_hi + EPS))
            qn = qp * r * gq
            qstack = jnp.concatenate([jnp.where(lo, qn, 0.0), jnp.where(lo, 0.0, qn)],
                                     axis=0).astype(BF16)
            tiles.append(_dot_nt(kdup[kv][band], qstack))
        return tiles

    def softmax(blk, grp, tiles):
        table = first if blk == 0 else 0
        pairs = []
        for pair in range(2):
            heads = []
            for pos in range(2):
                hd = SWA_GROUP * grp + 2 * pair + pos
                sink = sinks_ref[hd]
                cols = slice(pos * BLOCK, (pos + 1) * BLOCK)
                st = tiles[pair]
                s = (jnp.where(prev_valid, st[:BLOCK, cols], st[BLOCK:, cols])
                     + bias_ref[table, hd])
                m = jnp.maximum(jnp.max(s, axis=0, keepdims=True), sink)
                e = jnp.exp(s - m)
                den = jnp.sum(e, axis=0, keepdims=True) + jnp.exp(sink - m)
                pn = e * (1.0 / den)
                heads.append(jnp.concatenate([jnp.where(prev_valid, pn, 0.0),
                                              jnp.where(prev_valid, 0.0, pn)],
                                             axis=0).astype(BF16))
            pairs.append(jnp.concatenate(heads, axis=1))
        return pairs

    def finish(blk, grp, pairs):
        band = slice(blk * BLOCK, (blk + 2) * BLOCK)
        kv = (SWA_GROUP * grp) // heads_per_kv
        vt_band = vt[kv * SWA_HEAD_DIM:(kv + 1) * SWA_HEAD_DIM, band]
        ot = [_dot(vt_band, pairs[pair]) for pair in range(2)]
        ot = jnp.concatenate([ot[0][:, :BLOCK], ot[0][:, BLOCK:],
                              ot[1][:, :BLOCK], ot[1][:, BLOCK:]], axis=0)
        o = ot.T
        gate = gate_slab[grp][blk * BLOCK:(blk + 1) * BLOCK]
        y_blocks[grp][blk] = (o * (gate * jax.nn.sigmoid(gate))).astype(BF16)

    def out_project(grp):
        y = jnp.concatenate(y_blocks[grp], axis=0)
        part = _dot_cols(y, w_out[grp * out_per_group:(grp + 1) * out_per_group])
        if grp == 0:
            o_ref[0] = x + part
        else:
            o_ref[0] += part

    items = [(blk, grp) for grp in range(n_groups) for blk in range(n_blocks)]
    last = len(items) - 1
    jobs = []
    for grp in range(n_groups):
        jobs.append((functools.partial(project_gate, grp), 0, grp * n_blocks + 1))
        if grp + 1 < n_groups:
            jobs.append((functools.partial(project_q, grp + 1), 0, (grp + 1) * n_blocks - 2))
        if grp >= 1:
            jobs.append((functools.partial(out_project, grp - 1), grp * n_blocks + 1, last))
    jobs += [(prefetch_h, 0, last), (prefetch_q0, n_blocks - 1, last)]
    fillers = {}
    slot = -1
    for n, (job, first_item, last_item) in enumerate(jobs):
        slot = max(slot + 1, first_item, (n * len(items)) // len(jobs))
        assert slot <= last_item, "filler after the last item it may run in"
        fillers[slot] = job
    q_slab[0] = q0_ref
    pending = None
    tiles = scores(*items[0])
    for idx, (blk, grp) in enumerate(items):
        next_tiles = scores(*items[idx + 1]) if idx + 1 < len(items) else None
        if idx in fillers:
            fillers[idx]()
        if pending is not None:
            finish(*pending)
        pending = (blk, grp, softmax(blk, grp, tiles))
        tiles = next_tiles
    finish(*pending)
    out_project(n_groups - 1)


def _swa_layer(x, g, w_in, g_q, sinks, w_out, layer, k, vt, bias):
    b, s, d = x.shape
    t = SWA_TILE
    in_blocks = [_weight_block(layer, d, 0, c) for c in range(SWA_IN_BLOCKS)]
    out_blocks = [_weight_block(layer, SLAB, grp, c)
                  for grp in range(SWA_GROUPS) for c in range(d // SLAB)]

    def prev_block(j):
        return jnp.maximum(j * (t // BLOCK) - 1, 0)

    k_cur = pl.BlockSpec((1, t, KV_WIDTH), lambda i, j: (i, j, 0))
    k_prev = pl.BlockSpec((1, BLOCK, KV_WIDTH), lambda i, j: (i, prev_block(j), 0))
    vt_cur = pl.BlockSpec((1, KV_WIDTH, t), lambda i, j: (i, 0, j))
    vt_prev = pl.BlockSpec((1, KV_WIDTH, BLOCK), lambda i, j: (i, 0, prev_block(j)))
    n_tiles = s // t
    return pl.pallas_call(
        _swa_kernel,
        grid=(b, n_tiles),
        in_specs=[
            pl.BlockSpec(memory_space=pltpu.SMEM),
            pl.BlockSpec((1, t, d), lambda i, j: (i, j, 0)),
            _next_tile_spec(b, n_tiles, t, d),
            _resident((1, d)),
            _resident((1, LANES)),
            *in_blocks, *out_blocks,
            k_cur, k_prev, vt_cur, vt_prev,
            _resident(bias.shape),
        ],
        out_specs=pl.BlockSpec((1, t, d), lambda i, j: (i, j, 0)),
        out_shape=jax.ShapeDtypeStruct(x.shape, x.dtype),
        scratch_shapes=[pltpu.VMEM((t, d), BF16), pltpu.VMEM((t, SLAB), F32)],
        compiler_params=pltpu.CompilerParams(
            dimension_semantics=("arbitrary", "arbitrary"),
            vmem_limit_bytes=VMEM_LIMIT_BYTES),
        name="swa_layer",
    )(sinks, x, x.reshape(b * n_tiles, t, d), g.reshape(1, d),
      jnp.tile(g_q, 2).reshape(1, LANES) * (SWA_HEAD_DIM ** -0.5),
      *([w_in] * SWA_IN_BLOCKS), *([w_out] * SWA_OUT_BLOCKS), k, k, vt, vt, bias)


def kernel(x, a_norm_g, a_w_in, a_w_out, kv_norm_g, w_kv, k_norm_g, rel_bias,
           b_norm_g, b_w_in, b_q_norm_g, b_sinks, b_w_out):
    s = x.shape[1]
    tables = _retention_tables(s)
    a_w_in, a_w_out = a_w_in.astype(BF16), a_w_out.astype(BF16)
    b_w_in, b_w_out = b_w_in.astype(BF16), b_w_out.astype(BF16)
    for layer in range(a_w_in.shape[0]):
        x = _retention_layer(x, a_norm_g[layer], a_w_in, a_w_out, layer, *tables)
    k, vt = _shared_kv(x, kv_norm_g, w_kv, k_norm_g)
    bias = _bias_tables(rel_bias)
    for layer in range(b_w_in.shape[0]):
        x = _swa_layer(x, b_norm_g[layer], b_w_in, b_q_norm_g[layer], b_sinks[layer], b_w_out,
                       layer, k, vt, bias)
    return x
```

```python
import functools
import math

import jax
import jax.numpy as jnp
from jax import lax
from jax.experimental import pallas as pl
from jax.experimental.pallas import tpu as pltpu

D_MODEL = 1024
RET_HEADS = 4
RET_QK_DIM = D_MODEL // RET_HEADS
RET_V_DIM = 2 * RET_QK_DIM
RET_V_TOTAL = RET_HEADS * RET_V_DIM
RET_CHUNK = 128
A_IN_WIDTH = 2 * D_MODEL + 2 * RET_V_TOTAL
SWA_HEAD_DIM = 64
SWA_Q_HEADS = D_MODEL // SWA_HEAD_DIM
SWA_KV_HEADS = SWA_Q_HEADS // 8
SWA_Q_WIDTH = SWA_Q_HEADS * SWA_HEAD_DIM
KV_WIDTH = SWA_KV_HEADS * SWA_HEAD_DIM
WINDOW = 128
BLOCK = 128
REL_BUCKETS = 32
REL_MAX_DIST = 128
EPS = 1e-6

LANES = 128
SLAB = 256
VMEM_LIMIT_BYTES = 56 * 1024 * 1024

RET_TILE = 512
KV_TILE = 2048
SWA_TILE = 1024
SWA_GROUP = 4

BF16 = jnp.bfloat16
F32 = jnp.float32


def _dot(a, b):
    return jnp.dot(a, b, preferred_element_type=F32)


def _dot_nt(a, b):
    return lax.dot_general(a, b, (((1,), (1,)), ((), ())), preferred_element_type=F32)


def _dot_tn(a, b):
    return lax.dot_general(a, b, (((0,), (0,)), ((), ())), preferred_element_type=F32)


def _dot_cols(a, w_refs):
    return jnp.concatenate([_dot(a, w[...]) for w in w_refs], axis=1)


def _rms(x):
    return x * lax.rsqrt(jnp.mean(x * x, axis=-1, keepdims=True) + EPS)


def _resident(shape):
    nd = len(shape)
    return pl.BlockSpec(shape, lambda *_: (0,) * nd, pipeline_mode=pl.Buffered(1))


def _next_tile_spec(b, n_tiles, t, d):
    last = b * n_tiles - 1
    return pl.BlockSpec((1, t, d), lambda i, j: (jnp.minimum(i * n_tiles + j + 1, last), 0, 0))


def _weight_block(layer, rows, row_block, col_block):
    return pl.BlockSpec((None, rows, SLAB), lambda *_: (layer, row_block, col_block),
                        pipeline_mode=pl.Buffered(1))


RET_IN_BLOCKS = A_IN_WIDTH // SLAB
RET_OUT_BLOCKS = RET_HEADS * (D_MODEL // SLAB)


def _ret_kernel(x_ref, xnext_ref, g_ref, *refs):
    w_in = refs[:RET_IN_BLOCKS]
    w_out = refs[RET_IN_BLOCKS:RET_IN_BLOCKS + RET_OUT_BLOCKS]
    (cos_ref, sin_ref, idec_ref, qdec_ref, kdec_ref, cdec_ref,
     o_ref, state_ref, h_ref) = refs[RET_IN_BLOCKS + RET_OUT_BLOCKS:]
    qk_blocks = D_MODEL // SLAB
    v_per_head = RET_V_DIM // SLAB
    out_per_head = D_MODEL // SLAB

    def wq(hd):
        return w_in[hd:hd + 1]

    def wk(hd):
        return w_in[qk_blocks + hd:qk_blocks + hd + 1]

    def wv(hd):
        start = 2 * qk_blocks + hd * v_per_head
        return w_in[start:start + v_per_head]

    def wgate(hd):
        start = 2 * qk_blocks + RET_HEADS * v_per_head + hd * v_per_head
        return w_in[start:start + v_per_head]

    @pl.when(pl.program_id(1) == 0)
    def _():
        state_ref[...] = jnp.zeros_like(state_ref)

    def normed(ref):
        return (_rms(ref[0]) * g_ref[...]).astype(BF16)

    @pl.when((pl.program_id(0) == 0) & (pl.program_id(1) == 0))
    def _():
        h_ref[...] = normed(x_ref)

    def prefetch_h():
        h_ref[...] = normed(xnext_ref)

    x = x_ref[0]
    cos = cos_ref[...]
    sin = sin_ref[...]
    n_chunks = x.shape[0] // RET_CHUNK
    even = (lax.broadcasted_iota(jnp.int32, (1, LANES), 1) & 1) == 0

    def rotate(t):
        pieces = []
        for c in range(0, t.shape[1], LANES):
            tc = t[:, c:c + LANES]
            pieces.append(jnp.where(even, pltpu.roll(tc, LANES - 1, 1), pltpu.roll(tc, 1, 1)))
        return t * cos + jnp.concatenate(pieces, axis=1) * sin

    proj = [dict() for _ in range(RET_HEADS)]

    def project_q(hd):
        proj[hd]["q"] = rotate(_dot_cols(h_ref[...], wq(hd))).astype(BF16)

    def project_k(hd):
        k = rotate(_dot_cols(h_ref[...], wk(hd))) * (RET_QK_DIM ** -0.5)
        proj[hd]["k"] = k.astype(BF16)
        kdec = jnp.concatenate([kdec_ref[hd]] * n_chunks, axis=0)
        proj[hd]["kd"] = (k * kdec).astype(BF16)

    def project_v(hd):
        proj[hd]["v"] = _dot_cols(h_ref[...], wv(hd)).astype(BF16)

    def project_gate(hd):
        proj[hd]["gate"] = _dot_cols(h_ref[...], wgate(hd))

    def finish(hd, outs, acc):
        o = _rms(jnp.concatenate(outs, axis=0))
        gate = proj[hd]["gate"]
        y = (o * (gate * jax.nn.sigmoid(gate))).astype(BF16)
        return acc + _dot_cols(y, w_out[hd * out_per_head:(hd + 1) * out_per_head])

    project_q(0)
    project_k(0)
    project_v(0)
    acc = x
    pending = None
    for hd in range(RET_HEADS):
        fillers = [functools.partial(project_gate, hd)]
        if hd + 1 < RET_HEADS:
            fillers += [functools.partial(f, hd + 1) for f in (project_q, project_k, project_v)]
        else:
            fillers += [prefetch_h]
        qb, kb, kd, v = proj[hd]["q"], proj[hd]["k"], proj[hd]["kd"], proj[hd]["v"]
        st = state_ref[hd]
        outs = []
        for c in range(n_chunks):
            rows = slice(c * RET_CHUNK, (c + 1) * RET_CHUNK)
            qc, kc, vc = qb[rows], kb[rows], v[rows]
            scores = _dot_nt(qc, kc) * idec_ref[hd]
            inter = _dot(qc, st.astype(BF16)) * qdec_ref[hd]
            st = st * cdec_ref[hd] + _dot_tn(kd[rows], vc)
            lo_f = (c * len(fillers)) // n_chunks
            hi_f = ((c + 1) * len(fillers)) // n_chunks
            for f in fillers[lo_f:hi_f]:
                f()
            if c == 0 and pending is not None:
                acc = finish(*pending, acc)
            outs.append(_dot(scores.astype(BF16), vc) + inter)
        state_ref[hd] = st
        pending = (hd, outs)
    o_ref[0] = finish(*pending, acc)


def _retention_layer(x, g, w_in, w_out, layer, cos, sin, idec, qdec, kdec, cdec):
    b, s, d = x.shape
    t = RET_TILE
    in_blocks = [_weight_block(layer, d, 0, c) for c in range(RET_IN_BLOCKS)]
    out_blocks = [_weight_block(layer, RET_V_DIM, hd, c)
                  for hd in range(RET_HEADS) for c in range(d // SLAB)]
    n_tiles = s // t
    return pl.pallas_call(
        _ret_kernel,
        grid=(b, n_tiles),
        in_specs=[
            pl.BlockSpec((1, t, d), lambda i, j: (i, j, 0)),
            _next_tile_spec(b, n_tiles, t, d),
            _resident((1, d)),
            *in_blocks, *out_blocks,
            pl.BlockSpec((t, RET_QK_DIM), lambda i, j: (j, 0)),
            pl.BlockSpec((t, RET_QK_DIM), lambda i, j: (j, 0)),
            _resident(idec.shape), _resident(qdec.shape), _resident(kdec.shape),
            _resident(cdec.shape),
        ],
        out_specs=pl.BlockSpec((1, t, d), lambda i, j: (i, j, 0)),
        out_shape=jax.ShapeDtypeStruct(x.shape, x.dtype),
        scratch_shapes=[pltpu.VMEM((RET_HEADS, RET_QK_DIM, RET_V_DIM), F32),
                        pltpu.VMEM((t, d), BF16)],
        compiler_params=pltpu.CompilerParams(
            dimension_semantics=("arbitrary", "arbitrary"),
            vmem_limit_bytes=VMEM_LIMIT_BYTES),
        name="retention_layer",
    )(x, x.reshape(b * n_tiles, t, d), g.reshape(1, d),
      *([w_in] * RET_IN_BLOCKS), *([w_out] * RET_OUT_BLOCKS),
      cos, sin, idec, qdec, kdec, cdec)


def _retention_tables(s):
    dk = RET_QK_DIM
    angle = 1.0 / (10000.0 ** jnp.linspace(0.0, 1.0, dk // 2, dtype=F32))
    angle = jnp.repeat(angle, 2)
    pos = jnp.arange(s, dtype=F32)[:, None]
    sign = jnp.where(jnp.arange(dk) % 2 == 0, -1.0, 1.0).astype(F32)
    sin, cos = jnp.sin(pos * angle) * sign, jnp.cos(pos * angle)
    c = RET_CHUNK
    log_gamma = jnp.log(1.0 - 2.0 ** (-5.0 - jnp.arange(RET_HEADS, dtype=F32)))
    idx = jnp.arange(c, dtype=F32)
    diff = idx[:, None] - idx[None, :]
    idec = jnp.where(diff[None] >= 0,
                     jnp.exp(jnp.maximum(diff, 0.0)[None] * log_gamma[:, None, None]), 0.0)
    qdec = jnp.exp((idx + 1.0)[None, :, None] * log_gamma[:, None, None])
    kdec = jnp.exp((c - 1.0 - idx)[None, :, None] * log_gamma[:, None, None])
    cdec = jnp.exp(c * log_gamma)[:, None, None]
    return cos, sin, idec.astype(F32), qdec, kdec, cdec


def _kv_kernel(x_ref, g_ref, w_ref, gk_ref, k_ref, vt_ref):
    h = (_rms(x_ref[0]) * g_ref[...]).astype(BF16)
    kv = _dot(h, w_ref[...])
    k, v = kv[:, :KV_WIDTH], kv[:, KV_WIDTH:]
    lo = lax.broadcasted_iota(jnp.int32, (1, KV_WIDTH), 1) < SWA_HEAD_DIM
    sq = k * k
    ms_lo = jnp.sum(jnp.where(lo, sq, 0.0), axis=-1, keepdims=True) / SWA_HEAD_DIM
    ms_hi = jnp.sum(jnp.where(lo, 0.0, sq), axis=-1, keepdims=True) / SWA_HEAD_DIM
    r = jnp.where(lo, lax.rsqrt(ms_lo + EPS), lax.rsqrt(ms_hi + EPS))
    k_ref[0] = k * r * gk_ref[...]
    vt_ref[0] = v.T


def _shared_kv(x, g, w_kv, g_k):
    b, s, d = x.shape
    t = KV_TILE
    return pl.pallas_call(
        _kv_kernel,
        grid=(b, s // t),
        in_specs=[
            pl.BlockSpec((1, t, d), lambda i, j: (i, j, 0)),
            _resident((1, d)),
            _resident(w_kv.shape),
            _resident((1, KV_WIDTH)),
        ],
        out_specs=[pl.BlockSpec((1, t, KV_WIDTH), lambda i, j: (i, j, 0)),
                   pl.BlockSpec((1, KV_WIDTH, t), lambda i, j: (i, 0, j))],
        out_shape=[jax.ShapeDtypeStruct((b, s, KV_WIDTH), F32),
                   jax.ShapeDtypeStruct((b, KV_WIDTH, s), F32)],
        compiler_params=pltpu.CompilerParams(
            dimension_semantics=("parallel", "parallel"),
            vmem_limit_bytes=VMEM_LIMIT_BYTES),
        name="shared_kv",
    )(x, g.reshape(1, d), w_kv.astype(BF16), jnp.tile(g_k, SWA_KV_HEADS).reshape(1, KV_WIDTH))


def _prev_key_valid():
    j = lax.broadcasted_iota(jnp.int32, (BLOCK, BLOCK), 0)
    i = lax.broadcasted_iota(jnp.int32, (BLOCK, BLOCK), 1)
    return j > i, j, i


def _bias_kernel(rel_ref, o_ref):
    hd = pl.program_id(0)
    prev, j, i = _prev_key_valid()
    dist = jnp.where(prev, i + BLOCK - j, i - j)
    max_exact = REL_BUCKETS // 2
    dist_f = jnp.maximum(dist, 1).astype(F32)
    large = max_exact + (jnp.log(dist_f / max_exact) / math.log(REL_MAX_DIST / max_exact)
                         * (REL_BUCKETS - max_exact)).astype(jnp.int32)
    large = jnp.minimum(large, REL_BUCKETS - 1)
    bucket = jnp.where(dist < max_exact, dist, large)
    bias = jnp.zeros((BLOCK, BLOCK), F32)
    for bk in range(REL_BUCKETS):
        bias = jnp.where(bucket == bk, rel_ref[bk, hd], bias)
    o_ref[0, 0] = bias
    o_ref[1, 0] = jnp.where(prev, jnp.float32(-jnp.inf), bias)


def _bias_tables(rel_bias):
    return pl.pallas_call(
        _bias_kernel,
        grid=(SWA_Q_HEADS,),
        in_specs=[pl.BlockSpec(memory_space=pltpu.SMEM)],
        out_specs=pl.BlockSpec((2, 1, BLOCK, BLOCK), lambda h: (0, h, 0, 0)),
        out_shape=jax.ShapeDtypeStruct((2, SWA_Q_HEADS, BLOCK, BLOCK), F32),
        name="rel_bias_tables",
    )(rel_bias)


SWA_GROUPS = SWA_Q_HEADS // SWA_GROUP
SWA_IN_BLOCKS = 2 * SWA_Q_WIDTH // SLAB
SWA_OUT_BLOCKS = SWA_GROUPS * (D_MODEL // SLAB)
assert SWA_GROUP * SWA_HEAD_DIM == SLAB


def _swa_kernel(sinks_ref, x_ref, xnext_ref, g_ref, gq_ref, *refs):
    w_in = refs[:SWA_IN_BLOCKS]
    w_out = refs[SWA_IN_BLOCKS:SWA_IN_BLOCKS + SWA_OUT_BLOCKS]
    (kc_ref, kp_ref, vtc_ref, vtp_ref, bias_ref,
     o_ref, h_ref, q0_ref) = refs[SWA_IN_BLOCKS + SWA_OUT_BLOCKS:]
    out_per_group = D_MODEL // SLAB

    def normed(ref):
        return (_rms(ref[0]) * g_ref[...]).astype(BF16)

    @pl.when((pl.program_id(0) == 0) & (pl.program_id(1) == 0))
    def _():
        h_ref[...] = normed(x_ref)
        q0_ref[...] = _dot(h_ref[...], w_in[0][...])

    def prefetch_h():
        h_ref[...] = normed(xnext_ref)

    def prefetch_q0():
        q0_ref[...] = _dot(h_ref[...], w_in[0][...])

    first = (pl.program_id(1) == 0).astype(jnp.int32)
    x = x_ref[0]
    n_blocks = x.shape[0] // BLOCK
    n_groups = SWA_GROUPS

    lo = lax.broadcasted_iota(jnp.int32, (1, LANES), 1) < SWA_HEAD_DIM
    lo_rows = lax.broadcasted_iota(jnp.int32, (LANES, 1), 0) < SWA_HEAD_DIM

    kall = jnp.concatenate([kp_ref[0], kc_ref[0]], axis=0)
    kswap = pltpu.roll(kall, SWA_HEAD_DIM, 1)
    kdup = [jnp.where(lo, kall, kswap).astype(BF16), jnp.where(lo, kswap, kall).astype(BF16)]
    vt = jnp.concatenate([vtp_ref[0], vtc_ref[0]], axis=1).astype(BF16)
    gq = gq_ref[...]
    prev_valid, _, _ = _prev_key_valid()

    heads_per_kv = SWA_Q_HEADS // SWA_KV_HEADS

    q_slab = [None] * n_groups
    gate_slab = [None] * n_groups
    y_blocks = [[None] * n_blocks for _ in range(n_groups)]

    def project_q(grp):
        q_slab[grp] = _dot(h_ref[...], w_in[grp][...])

    def project_gate(grp):
        gate_slab[grp] = _dot(h_ref[...], w_in[n_groups + grp][...])

    def scores(blk, grp):
        band = slice(blk * BLOCK, (blk + 2) * BLOCK)
        kv = (SWA_GROUP * grp) // heads_per_kv
        tiles = []
        for pair in range(2):
            qp = q_slab[grp][blk * BLOCK:(blk + 1) * BLOCK, pair * LANES:(pair + 1) * LANES]
            sq = qp * qp
            ms_lo = jnp.sum(jnp.where(lo, sq, 0.0), axis=-1, keepdims=True) / SWA_HEAD_DIM
            ms_hi = jnp.sum(jnp.where(lo, 0.0, sq), axis=-1, keepdims=True) / SWA_HEAD_DIM
            r = jnp.where(lo, lax.rsqrt(ms_lo + EPS), lax.rsqrt(ms_hi + EPS))
            qn = qp * r * gq
            qt = qn.T
            qstack = jnp.concatenate([jnp.where(lo_rows, qt, 0.0), jnp.where(lo_rows, 0.0, qt)],
                                     axis=1).astype(BF16)
            tiles.append(_dot(kdup[kv][band], qstack))
        return tiles

    def softmax(blk, grp, tiles):
        table = first if blk == 0 else 0
        pairs = []
        for pair in range(2):
            heads = []
            for pos in range(2):
                hd = SWA_GROUP * grp + 2 * pair + pos
                sink = sinks_ref[hd]
                cols = slice(pos * BLOCK, (pos + 1) * BLOCK)
                st = tiles[pair]
                s = (jnp.where(prev_valid, st[:BLOCK, cols], st[BLOCK:, cols])
                     + bias_ref[table, hd])
                m = jnp.maximum(jnp.max(s, axis=0, keepdims=True), sink)
                e = jnp.exp(s - m)
                den = jnp.sum(e, axis=0, keepdims=True) + jnp.exp(sink - m)
                pn = e * (1.0 / den)
                heads.append(jnp.concatenate([jnp.where(prev_valid, pn, 0.0),
                                              jnp.where(prev_valid, 0.0, pn)],
                                             axis=0).astype(BF16))
            pairs.append(jnp.concatenate(heads, axis=1))
        return pairs

    def finish(blk, grp, pairs):
        band = slice(blk * BLOCK, (blk + 2) * BLOCK)
        kv = (SWA_GROUP * grp) // heads_per_kv
        vt_band = vt[kv * SWA_HEAD_DIM:(kv + 1) * SWA_HEAD_DIM, band]
        ot = [_dot(vt_band, pairs[pair]) for pair in range(2)]
        ot = jnp.concatenate([ot[0][:, :BLOCK], ot[0][:, BLOCK:],
                              ot[1][:, :BLOCK], ot[1][:, BLOCK:]], axis=0)
        o = ot.T
        gate = gate_slab[grp][blk * BLOCK:(blk + 1) * BLOCK]
        y_blocks[grp][blk] = (o * (gate * jax.nn.sigmoid(gate))).astype(BF16)

    def out_project(grp):
        y = jnp.concatenate(y_blocks[grp], axis=0)
        part = _dot_cols(y, w_out[grp * out_per_group:(grp + 1) * out_per_group])
        if grp == 0:
            o_ref[0] = x + part
        else:
            o_ref[0] += part

    items = [(blk, grp) for grp in range(n_groups) for blk in range(n_blocks)]
    last = len(items) - 1
    jobs = []
    for grp in range(n_groups):
        jobs.append((functools.partial(project_gate, grp), 0, grp * n_blocks + 1))
        if grp + 1 < n_groups:
            jobs.append((functools.partial(project_q, grp + 1), 0, (grp + 1) * n_blocks - 2))
        if grp >= 1:
            jobs.append((functools.partial(out_project, grp - 1), grp * n_blocks + 1, last))
    jobs += [(prefetch_h, 0, last), (prefetch_q0, n_blocks - 1, last)]
    fillers = {}
    slot = -1
    for n, (job, first_item, last_item) in enumerate(jobs):
        slot = max(slot + 1, first_item, (n * len(items)) // len(jobs))
        assert slot <= last_item, "filler after the last item it may run in"
        fillers[slot] = job
    q_slab[0] = q0_ref
    pending = None
    tiles = scores(*items[0])
    for idx, (blk, grp) in enumerate(items):
        next_tiles = scores(*items[idx + 1]) if idx + 1 < len(items) else None
        if idx in fillers:
            fillers[idx]()
        if pending is not None:
            finish(*pending)
        pending = (blk, grp, softmax(blk, grp, tiles))
        tiles = next_tiles
    finish(*pending)
    out_project(n_groups - 1)


def _swa_layer(x, g, w_in, g_q, sinks, w_out, layer, k, vt, bias):
    b, s, d = x.shape
    t = SWA_TILE
    in_blocks = [_weight_block(layer, d, 0, c) for c in range(SWA_IN_BLOCKS)]
    out_blocks = [_weight_block(layer, SLAB, grp, c)
                  for grp in range(SWA_GROUPS) for c in range(d // SLAB)]

    def prev_block(j):
        return jnp.maximum(j * (t // BLOCK) - 1, 0)

    k_cur = pl.BlockSpec((1, t, KV_WIDTH), lambda i, j: (i, j, 0))
    k_prev = pl.BlockSpec((1, BLOCK, KV_WIDTH), lambda i, j: (i, prev_block(j), 0))
    vt_cur = pl.BlockSpec((1, KV_WIDTH, t), lambda i, j: (i, 0, j))
    vt_prev = pl.BlockSpec((1, KV_WIDTH, BLOCK), lambda i, j: (i, 0, prev_block(j)))
    n_tiles = s // t
    return pl.pallas_call(
        _swa_kernel,
        grid=(b, n_tiles),
        in_specs=[
            pl.BlockSpec(memory_space=pltpu.SMEM),
            pl.BlockSpec((1, t, d), lambda i, j: (i, j, 0)),
            _next_tile_spec(b, n_tiles, t, d),
            _resident((1, d)),
            _resident((1, LANES)),
            *in_blocks, *out_blocks,
            k_cur, k_prev, vt_cur, vt_prev,
            _resident(bias.shape),
        ],
        out_specs=pl.BlockSpec((1, t, d), lambda i, j: (i, j, 0)),
        out_shape=jax.ShapeDtypeStruct(x.shape, x.dtype),
        scratch_shapes=[pltpu.VMEM((t, d), BF16), pltpu.VMEM((t, SLAB), F32)],
        compiler_params=pltpu.CompilerParams(
            dimension_semantics=("arbitrary", "arbitrary"),
            vmem_limit_bytes=VMEM_LIMIT_BYTES),
        name="swa_layer",
    )(sinks, x, x.reshape(b * n_tiles, t, d), g.reshape(1, d),
      jnp.tile(g_q, 2).reshape(1, LANES) * (SWA_HEAD_DIM ** -0.5),
      *([w_in] * SWA_IN_BLOCKS), *([w_out] * SWA_OUT_BLOCKS), k, k, vt, vt, bias)


def kernel(x, a_norm_g, a_w_in, a_w_out, kv_norm_g, w_kv, k_norm_g, rel_bias,
           b_norm_g, b_w_in, b_q_norm_g, b_sinks, b_w_out):
    s = x.shape[1]
    tables = _retention_tables(s)
    a_w_in, a_w_out = a_w_in.astype(BF16), a_w_out.astype(BF16)
    b_w_in, b_w_out = b_w_in.astype(BF16), b_w_out.astype(BF16)
    for layer in range(a_w_in.shape[0]):
        x = _retention_layer(x, a_norm_g[layer], a_w_in, a_w_out, layer, *tables)
    k, vt = _shared_kv(x, kv_norm_g, w_kv, k_norm_g)
    bias = _bias_tables(rel_bias)
    for layer in range(b_w_in.shape[0]):
        x = _swa_layer(x, b_norm_g[layer], b_w_in, b_q_norm_g[layer], b_sinks[layer], b_w_out,
                       layer, k, vt, bias)
    return x
```

```python
import functools
import math

import jax
import jax.numpy as jnp
from jax import lax
from jax.experimental import pallas as pl
from jax.experimental.pallas import tpu as pltpu

D_MODEL = 1024
RET_HEADS = 4
RET_QK_DIM = D_MODEL // RET_HEADS
RET_V_DIM = 2 * RET_QK_DIM
RET_V_TOTAL = RET_HEADS * RET_V_DIM
RET_CHUNK = 128
A_IN_WIDTH = 2 * D_MODEL + 2 * RET_V_TOTAL
SWA_HEAD_DIM = 64
SWA_Q_HEADS = D_MODEL // SWA_HEAD_DIM
SWA_KV_HEADS = SWA_Q_HEADS // 8
SWA_Q_WIDTH = SWA_Q_HEADS * SWA_HEAD_DIM
KV_WIDTH = SWA_KV_HEADS * SWA_HEAD_DIM
WINDOW = 128
BLOCK = 128
REL_BUCKETS = 32
REL_MAX_DIST = 128
EPS = 1e-6

LANES = 128
SLAB = 256
VMEM_LIMIT_BYTES = 56 * 1024 * 1024

RET_TILE = 512
KV_TILE = 2048
SWA_TILE = 1024
SWA_GROUP = 4

BF16 = jnp.bfloat16
F32 = jnp.float32


def _dot(a, b):
    return jnp.dot(a, b, preferred_element_type=F32)


def _dot_nt(a, b):
    return lax.dot_general(a, b, (((1,), (1,)), ((), ())), preferred_element_type=F32)


def _dot_tn(a, b):
    return lax.dot_general(a, b, (((0,), (0,)), ((), ())), preferred_element_type=F32)


def _dot_cols(a, w_refs):
    return jnp.concatenate([_dot(a, w[...]) for w in w_refs], axis=1)


def _rms(x):
    return x * lax.rsqrt(jnp.mean(x * x, axis=-1, keepdims=True) + EPS)


def _resident(shape):
    nd = len(shape)
    return pl.BlockSpec(shape, lambda *_: (0,) * nd, pipeline_mode=pl.Buffered(1))


def _next_tile_spec(b, n_tiles, t, d):
    last = b * n_tiles - 1
    return pl.BlockSpec((1, t, d), lambda i, j: (jnp.minimum(i * n_tiles + j + 1, last), 0, 0))


def _weight_block(layer, rows, row_block, col_block):
    return pl.BlockSpec((None, rows, SLAB), lambda *_: (layer, row_block, col_block),
                        pipeline_mode=pl.Buffered(1))


RET_IN_BLOCKS = A_IN_WIDTH // SLAB
RET_OUT_BLOCKS = RET_HEADS * (D_MODEL // SLAB)


def _ret_kernel(x_ref, xnext_ref, g_ref, *refs):
    w_in = refs[:RET_IN_BLOCKS]
    w_out = refs[RET_IN_BLOCKS:RET_IN_BLOCKS + RET_OUT_BLOCKS]
    (cos_ref, sin_ref, idec_ref, qdec_ref, kdec_ref, cdec_ref,
     o_ref, state_ref, h_ref) = refs[RET_IN_BLOCKS + RET_OUT_BLOCKS:]
    qk_blocks = D_MODEL // SLAB
    v_per_head = RET_V_DIM // SLAB
    out_per_head = D_MODEL // SLAB

    def wq(hd):
        return w_in[hd:hd + 1]

    def wk(hd):
        return w_in[qk_blocks + hd:qk_blocks + hd + 1]

    def wv(hd):
        start = 2 * qk_blocks + hd * v_per_head
        return w_in[start:start + v_per_head]

    def wgate(hd):
        start = 2 * qk_blocks + RET_HEADS * v_per_head + hd * v_per_head
        return w_in[start:start + v_per_head]

    @pl.when(pl.program_id(1) == 0)
    def _():
        state_ref[...] = jnp.zeros_like(state_ref)

    def normed(ref):
        return (_rms(ref[0]) * g_ref[...]).astype(BF16)

    @pl.when((pl.program_id(0) == 0) & (pl.program_id(1) == 0))
    def _():
        h_ref[...] = normed(x_ref)

    def prefetch_h():
        h_ref[...] = normed(xnext_ref)

    x = x_ref[0]
    cos = cos_ref[...]
    sin = sin_ref[...]
    n_chunks = x.shape[0] // RET_CHUNK
    even = (lax.broadcasted_iota(jnp.int32, (1, LANES), 1) & 1) == 0

    def rotate(t):
        pieces = []
        for c in range(0, t.shape[1], LANES):
            tc = t[:, c:c + LANES]
            pieces.append(jnp.where(even, pltpu.roll(tc, LANES - 1, 1), pltpu.roll(tc, 1, 1)))
        return t * cos + jnp.concatenate(pieces, axis=1) * sin

    proj = [dict() for _ in range(RET_HEADS)]

    def project_q(hd):
        proj[hd]["q"] = rotate(_dot_cols(h_ref[...], wq(hd))).astype(BF16)

    def project_k(hd):
        k = rotate(_dot_cols(h_ref[...], wk(hd))) * (RET_QK_DIM ** -0.5)
        proj[hd]["k"] = k.astype(BF16)
        kdec = jnp.concatenate([kdec_ref[hd]] * n_chunks, axis=0)
        proj[hd]["kd"] = (k * kdec).astype(BF16)

    def project_v(hd):
        proj[hd]["v"] = _dot_cols(h_ref[...], wv(hd)).astype(BF16)

    def project_gate(hd):
        proj[hd]["gate"] = _dot_cols(h_ref[...], wgate(hd))

    def finish(hd, outs, acc):
        o = _rms(jnp.concatenate(outs, axis=0))
        gate = proj[hd]["gate"]
        y = (o * (gate * jax.nn.sigmoid(gate))).astype(BF16)
        return acc + _dot_cols(y, w_out[hd * out_per_head:(hd + 1) * out_per_head])

    project_q(0)
    project_k(0)
    project_v(0)
    acc = x
    pending = None
    for hd in range(RET_HEADS):
        fillers = [functools.partial(project_gate, hd)]
        if hd + 1 < RET_HEADS:
            fillers += [functools.partial(f, hd + 1) for f in (project_q, project_k, project_v)]
        else:
            fillers += [prefetch_h]
        qb, kb, kd, v = proj[hd]["q"], proj[hd]["k"], proj[hd]["kd"], proj[hd]["v"]
        st = state_ref[hd]
        outs = []
        for c in range(n_chunks):
            rows = slice(c * RET_CHUNK, (c + 1) * RET_CHUNK)
            qc, kc, vc = qb[rows], kb[rows], v[rows]
            scores = _dot_nt(qc, kc) * idec_ref[hd]
            inter = _dot(qc, st.astype(BF16)) * qdec_ref[hd]
            st = st * cdec_ref[hd] + _dot_tn(kd[rows], vc)
            lo_f = (c * len(fillers)) // n_chunks
            hi_f = ((c + 1) * len(fillers)) // n_chunks
            for f in fillers[lo_f:hi_f]:
                f()
            if c == 0 and pending is not None:
                acc = finish(*pending, acc)
            outs.append(_dot(scores.astype(BF16), vc) + inter)
        state_ref[hd] = st
        pending = (hd, outs)
    o_ref[0] = finish(*pending, acc)


def _retention_layer(x, g, w_in, w_out, layer, cos, sin, idec, qdec, kdec, cdec):
    b, s, d = x.shape
    t = RET_TILE
    in_blocks = [_weight_block(layer, d, 0, c) for c in range(RET_IN_BLOCKS)]
    out_blocks = [_weight_block(layer, RET_V_DIM, hd, c)
                  for hd in range(RET_HEADS) for c in range(d // SLAB)]
    n_tiles = s // t
    return pl.pallas_call(
        _ret_kernel,
        grid=(b, n_tiles),
        in_specs=[
            pl.BlockSpec((1, t, d), lambda i, j: (i, j, 0)),
            _next_tile_spec(b, n_tiles, t, d),
            _resident((1, d)),
            *in_blocks, *out_blocks,
            pl.BlockSpec((t, RET_QK_DIM), lambda i, j: (j, 0)),
            pl.BlockSpec((t, RET_QK_DIM), lambda i, j: (j, 0)),
            _resident(idec.shape), _resident(qdec.shape), _resident(kdec.shape),
            _resident(cdec.shape),
        ],
        out_specs=pl.BlockSpec((1, t, d), lambda i, j: (i, j, 0)),
        out_shape=jax.ShapeDtypeStruct(x.shape, x.dtype),
        scratch_shapes=[pltpu.VMEM((RET_HEADS, RET_QK_DIM, RET_V_DIM), F32),
                        pltpu.VMEM((t, d), BF16)],
        compiler_params=pltpu.CompilerParams(
            dimension_semantics=("arbitrary", "arbitrary"),
            vmem_limit_bytes=VMEM_LIMIT_BYTES),
        name="retention_layer",
    )(x, x.reshape(b * n_tiles, t, d), g.reshape(1, d),
      *([w_in] * RET_IN_BLOCKS), *([w_out] * RET_OUT_BLOCKS),
      cos, sin, idec, qdec, kdec, cdec)


def _retention_tables(s):
    dk = RET_QK_DIM
    angle = 1.0 / (10000.0 ** jnp.linspace(0.0, 1.0, dk // 2, dtype=F32))
    angle = jnp.repeat(angle, 2)
    pos = jnp.arange(s, dtype=F32)[:, None]
    sign = jnp.where(jnp.arange(dk) % 2 == 0, -1.0, 1.0).astype(F32)
    sin, cos = jnp.sin(pos * angle) * sign, jnp.cos(pos * angle)
    c = RET_CHUNK
    log_gamma = jnp.log(1.0 - 2.0 ** (-5.0 - jnp.arange(RET_HEADS, dtype=F32)))
    idx = jnp.arange(c, dtype=F32)
    diff = idx[:, None] - idx[None, :]
    idec = jnp.where(diff[None] >= 0,
                     jnp.exp(jnp.maximum(diff, 0.0)[None] * log_gamma[:, None, None]), 0.0)
    qdec = jnp.exp((idx + 1.0)[None, :, None] * log_gamma[:, None, None])
    kdec = jnp.exp((c - 1.0 - idx)[None, :, None] * log_gamma[:, None, None])
    cdec = jnp.exp(c * log_gamma)[:, None, None]
    return cos, sin, idec.astype(F32), qdec, kdec, cdec


def _kv_kernel(x_ref, g_ref, w_ref, gk_ref, k_ref, vt_ref):
    h = (_rms(x_ref[0]) * g_ref[...]).astype(BF16)
    kv = _dot(h, w_ref[...])
    k, v = kv[:, :KV_WIDTH], kv[:, KV_WIDTH:]
    lo = lax.broadcasted_iota(jnp.int32, (1, KV_WIDTH), 1) < SWA_HEAD_DIM
    sq = k * k
    ms_lo = jnp.sum(jnp.where(lo, sq, 0.0), axis=-1, keepdims=True) / SWA_HEAD_DIM
    ms_hi = jnp.sum(jnp.where(lo, 0.0, sq), axis=-1, keepdims=True) / SWA_HEAD_DIM
    r = jnp.where(lo, lax.rsqrt(ms_lo + EPS), lax.rsqrt(ms_hi + EPS))
    k_ref[0] = k * r * gk_ref[...]
    vt_ref[0] = v.T


def _shared_kv(x, g, w_kv, g_k):
    b, s, d = x.shape
    t = KV_TILE
    return pl.pallas_call(
        _kv_kernel,
        grid=(b, s // t),
        in_specs=[
            pl.BlockSpec((1, t, d), lambda i, j: (i, j, 0)),
            _resident((1, d)),
            _resident(w_kv.shape),
            _resident((1, KV_WIDTH)),
        ],
        out_specs=[pl.BlockSpec((1, t, KV_WIDTH), lambda i, j: (i, j, 0)),
                   pl.BlockSpec((1, KV_WIDTH, t), lambda i, j: (i, 0, j))],
        out_shape=[jax.ShapeDtypeStruct((b, s, KV_WIDTH), F32),
                   jax.ShapeDtypeStruct((b, KV_WIDTH, s), F32)],
        compiler_params=pltpu.CompilerParams(
            dimension_semantics=("parallel", "parallel"),
            vmem_limit_bytes=VMEM_LIMIT_BYTES),
        name="shared_kv",
    )(x, g.reshape(1, d), w_kv.astype(BF16), jnp.tile(g_k, SWA_KV_HEADS).reshape(1, KV_WIDTH))


def _prev_key_valid():
    j = lax.broadcasted_iota(jnp.int32, (BLOCK, BLOCK), 0)
    i = lax.broadcasted_iota(jnp.int32, (BLOCK, BLOCK), 1)
    return j > i, j, i


def _bias_kernel(rel_ref, o_ref):
    hd = pl.program_id(0)
    prev, j, i = _prev_key_valid()
    dist = jnp.where(prev, i + BLOCK - j, i - j)
    max_exact = REL_BUCKETS // 2
    dist_f = jnp.maximum(dist, 1).astype(F32)
    large = max_exact + jnp.floor(jnp.log(dist_f / max_exact) / math.log(REL_MAX_DIST / max_exact)
                                  * (REL_BUCKETS - max_exact)).astype(jnp.int32)
    large = jnp.minimum(large, REL_BUCKETS - 1)
    bucket = jnp.where(dist < max_exact, dist, large)
    bias = jnp.zeros((BLOCK, BLOCK), F32)
    for bk in range(REL_BUCKETS):
        bias = jnp.where(bucket == bk, rel_ref[bk, hd], bias)
    o_ref[0, 0] = bias
    o_ref[1, 0] = jnp.where(prev, jnp.float32(-jnp.inf), bias)


def _bias_tables(rel_bias):
    return pl.pallas_call(
        _bias_kernel,
        grid=(SWA_Q_HEADS,),
        in_specs=[pl.BlockSpec(memory_space=pltpu.SMEM)],
        out_specs=pl.BlockSpec((2, 1, BLOCK, BLOCK), lambda h: (0, h, 0, 0)),
        out_shape=jax.ShapeDtypeStruct((2, SWA_Q_HEADS, BLOCK, BLOCK), F32),
        name="rel_bias_tables",
    )(rel_bias)


SWA_GROUPS = SWA_Q_HEADS // SWA_GROUP
SWA_IN_BLOCKS = 2 * SWA_Q_WIDTH // SLAB
SWA_OUT_BLOCKS = SWA_GROUPS * (D_MODEL // SLAB)
assert SWA_GROUP * SWA_HEAD_DIM == SLAB


def _swa_kernel(sinks_ref, x_ref, xnext_ref, g_ref, gq_ref, *refs):
    w_in = refs[:SWA_IN_BLOCKS]
    w_out = refs[SWA_IN_BLOCKS:SWA_IN_BLOCKS + SWA_OUT_BLOCKS]
    (kc_ref, kp_ref, vtc_ref, vtp_ref, bias_ref,
     o_ref, h_ref, q0_ref) = refs[SWA_IN_BLOCKS + SWA_OUT_BLOCKS:]
    out_per_group = D_MODEL // SLAB

    def normed(ref):
        return (_rms(ref[0]) * g_ref[...]).astype(BF16)

    @pl.when((pl.program_id(0) == 0) & (pl.program_id(1) == 0))
    def _():
        h_ref[...] = normed(x_ref)
        q0_ref[...] = _dot(h_ref[...], w_in[0][...])

    def prefetch_h():
        h_ref[...] = normed(xnext_ref)

    def prefetch_q0():
        q0_ref[...] = _dot(h_ref[...], w_in[0][...])

    first = (pl.program_id(1) == 0).astype(jnp.int32)
    x = x_ref[0]
    n_blocks = x.shape[0] // BLOCK
    n_groups = SWA_GROUPS

    lo = lax.broadcasted_iota(jnp.int32, (1, LANES), 1) < SWA_HEAD_DIM
    lo_rows = lax.broadcasted_iota(jnp.int32, (LANES, 1), 0) < SWA_HEAD_DIM

    kall = jnp.concatenate([kp_ref[0], kc_ref[0]], axis=0)
    kswap = pltpu.roll(kall, SWA_HEAD_DIM, 1)
    kdup = [jnp.where(lo, kall, kswap).astype(BF16), jnp.where(lo, kswap, kall).astype(BF16)]
    vt = jnp.concatenate([vtp_ref[0], vtc_ref[0]], axis=1).astype(BF16)
    gq = gq_ref[...]
    prev_valid, _, _ = _prev_key_valid()

    heads_per_kv = SWA_Q_HEADS // SWA_KV_HEADS

    q_slab = [None] * n_groups
    gate_slab = [None] * n_groups
    y_blocks = [[None] * n_blocks for _ in range(n_groups)]

    def project_q(grp):
        q_slab[grp] = _dot(h_ref[...], w_in[grp][...])

    def project_gate(grp):
        gate_slab[grp] = _dot(h_ref[...], w_in[n_groups + grp][...])

    def scores(blk, grp):
        band = slice(blk * BLOCK, (blk + 2) * BLOCK)
        kv = (SWA_GROUP * grp) // heads_per_kv
        tiles = []
        for pair in range(2):
            qp = q_slab[grp][blk * BLOCK:(blk + 1) * BLOCK, pair * LANES:(pair + 1) * LANES]
            sq = qp * qp
            ms_lo = jnp.sum(jnp.where(lo, sq, 0.0), axis=-1, keepdims=True) / SWA_HEAD_DIM
            ms_hi = jnp.sum(jnp.where(lo, 0.0, sq), axis=-1, keepdims=True) / SWA_HEAD_DIM
            r = jnp.where(lo, lax.rsqrt(ms_lo + EPS), lax.rsqrt(ms_hi + EPS))
            qn = qp * r * gq
            qt = qn.T
            qstack = jnp.concatenate([jnp.where(lo_rows, qt, 0.0), jnp.where(lo_rows, 0.0, qt)],
                                     axis=1).astype(BF16)
            tiles.append(_dot(kdup[kv][band], qstack))
        return tiles

    def softmax(blk, grp, tiles):
        table = first if blk == 0 else 0
        pairs = []
        for pair in range(2):
            heads = []
            for pos in range(2):
                hd = SWA_GROUP * grp + 2 * pair + pos
                sink = sinks_ref[hd]
                cols = slice(pos * BLOCK, (pos + 1) * BLOCK)
                st = tiles[pair]
                s = (jnp.where(prev_valid, st[:BLOCK, cols], st[BLOCK:, cols])
                     + bias_ref[table, hd])
                m = jnp.maximum(jnp.max(s, axis=0, keepdims=True), sink)
                e = jnp.exp(s - m)
                den = jnp.sum(e, axis=0, keepdims=True) + jnp.exp(sink - m)
                pn = e * (1.0 / den)
                heads.append(jnp.concatenate([jnp.where(prev_valid, pn, 0.0),
                                              jnp.where(prev_valid, 0.0, pn)],
                                             axis=0).astype(BF16))
            pairs.append(jnp.concatenate(heads, axis=1))
        return pairs

    def finish(blk, grp, pairs):
        band = slice(blk * BLOCK, (blk + 2) * BLOCK)
        kv = (SWA_GROUP * grp) // heads_per_kv
        vt_band = vt[kv * SWA_HEAD_DIM:(kv + 1) * SWA_HEAD_DIM, band]
        ot = [_dot(vt_band, pairs[pair]) for pair in range(2)]
        ot = jnp.concatenate([ot[0][:, :BLOCK], ot[0][:, BLOCK:],
                              ot[1][:, :BLOCK], ot[1][:, BLOCK:]], axis=0)
        o = ot.T
        gate = gate_slab[grp][blk * BLOCK:(blk + 1) * BLOCK]
        y_blocks[grp][blk] = (o * (gate * jax.nn.sigmoid(gate))).astype(BF16)

    def out_project(grp):
        y = jnp.concatenate(y_blocks[grp], axis=0)
        part = _dot_cols(y, w_out[grp * out_per_group:(grp + 1) * out_per_group])
        if grp == 0:
            o_ref[0] = x + part
        else:
            o_ref[0] += part

    items = [(blk, grp) for grp in range(n_groups) for blk in range(n_blocks)]
    last = len(items) - 1
    jobs = []
    for grp in range(n_groups):
        jobs.append((functools.partial(project_gate, grp), 0, grp * n_blocks + 1))
        if grp + 1 < n_groups:
            jobs.append((functools.partial(project_q, grp + 1), 0, (grp + 1) * n_blocks - 2))
        if grp >= 1:
            jobs.append((functools.partial(out_project, grp - 1), grp * n_blocks + 1, last))
    jobs += [(prefetch_h, 0, last), (prefetch_q0, n_blocks - 1, last)]
    pairs = [jobs[n:n + 2] for n in range(0, len(jobs), 2)]
    fillers = {}
    slot = -1
    for n, pair in enumerate(pairs):
        slot = max(slot + 1, (n * len(items)) // len(pairs), *(first for _, first, _ in pair))
        assert all(slot <= last_item for _, _, last_item in pair), "filler after its consumer"
        fillers[slot] = [job for job, _, _ in pair]
    q_slab[0] = q0_ref
    pending = None
    for job in fillers.pop(0, []):
        job()
    tiles = scores(*items[0])
    for idx, (blk, grp) in enumerate(items):
        next_tiles = scores(*items[idx + 1]) if idx + 1 < len(items) else None
        for job in fillers.get(idx, []):
            job()
        if pending is not None:
            finish(*pending)
        pending = (blk, grp, softmax(blk, grp, tiles))
        tiles = next_tiles
    finish(*pending)
    out_project(n_groups - 1)


def _swa_layer(x, g, w_in, g_q, sinks, w_out, layer, k, vt, bias):
    b, s, d = x.shape
    t = SWA_TILE
    in_blocks = [_weight_block(layer, d, 0, c) for c in range(SWA_IN_BLOCKS)]
    out_blocks = [_weight_block(layer, SLAB, grp, c)
                  for grp in range(SWA_GROUPS) for c in range(d // SLAB)]

    def prev_block(j):
        return jnp.maximum(j * (t // BLOCK) - 1, 0)

    k_cur = pl.BlockSpec((1, t, KV_WIDTH), lambda i, j: (i, j, 0))
    k_prev = pl.BlockSpec((1, BLOCK, KV_WIDTH), lambda i, j: (i, prev_block(j), 0))
    vt_cur = pl.BlockSpec((1, KV_WIDTH, t), lambda i, j: (i, 0, j))
    vt_prev = pl.BlockSpec((1, KV_WIDTH, BLOCK), lambda i, j: (i, 0, prev_block(j)))
    n_tiles = s // t
    return pl.pallas_call(
        _swa_kernel,
        grid=(b, n_tiles),
        in_specs=[
            pl.BlockSpec(memory_space=pltpu.SMEM),
            pl.BlockSpec((1, t, d), lambda i, j: (i, j, 0)),
            _next_tile_spec(b, n_tiles, t, d),
            _resident((1, d)),
            _resident((1, LANES)),
            *in_blocks, *out_blocks,
            k_cur, k_prev, vt_cur, vt_prev,
            _resident(bias.shape),
        ],
        out_specs=pl.BlockSpec((1, t, d), lambda i, j: (i, j, 0)),
        out_shape=jax.ShapeDtypeStruct(x.shape, x.dtype),
        scratch_shapes=[pltpu.VMEM((t, d), BF16), pltpu.VMEM((t, SLAB), F32)],
        compiler_params=pltpu.CompilerParams(
            dimension_semantics=("arbitrary", "arbitrary"),
            vmem_limit_bytes=VMEM_LIMIT_BYTES),
        name="swa_layer",
    )(sinks, x, x.reshape(b * n_tiles, t, d), g.reshape(1, d),
      jnp.tile(g_q, 2).reshape(1, LANES) * (SWA_HEAD_DIM ** -0.5),
      *([w_in] * SWA_IN_BLOCKS), *([w_out] * SWA_OUT_BLOCKS), k, k, vt, vt, bias)


def kernel(x, a_norm_g, a_w_in, a_w_out, kv_norm_g, w_kv, k_norm_g, rel_bias,
           b_norm_g, b_w_in, b_q_norm_g, b_sinks, b_w_out):
    s = x.shape[1]
    tables = _retention_tables(s)
    a_w_in, a_w_out = a_w_in.astype(BF16), a_w_out.astype(BF16)
    b_w_in, b_w_out = b_w_in.astype(BF16), b_w_out.astype(BF16)
    for layer in range(a_w_in.shape[0]):
        x = _retention_layer(x, a_norm_g[layer], a_w_in, a_w_out, layer, *tables)
    k, vt = _shared_kv(x, kv_norm_g, w_kv, k_norm_g)
    bias = _bias_tables(rel_bias)
    for layer in range(b_w_in.shape[0]):
        x = _swa_layer(x, b_norm_g[layer], b_w_in, b_q_norm_g[layer], b_sinks[layer], b_w_out,
                       layer, k, vt, bias)
    return x
```

```python
import functools
import math

import jax
import jax.numpy as jnp
from jax import lax
from jax.experimental import pallas as pl
from jax.experimental.pallas import tpu as pltpu

D_MODEL = 1024
RET_HEADS = 4
RET_QK_DIM = D_MODEL // RET_HEADS
RET_V_DIM = 2 * RET_QK_DIM
RET_V_TOTAL = RET_HEADS * RET_V_DIM
RET_CHUNK = 128
A_IN_WIDTH = 2 * D_MODEL + 2 * RET_V_TOTAL
SWA_HEAD_DIM = 64
SWA_Q_HEADS = D_MODEL // SWA_HEAD_DIM
SWA_KV_HEADS = SWA_Q_HEADS // 8
SWA_Q_WIDTH = SWA_Q_HEADS * SWA_HEAD_DIM
KV_WIDTH = SWA_KV_HEADS * SWA_HEAD_DIM
WINDOW = 128
BLOCK = 128
REL_BUCKETS = 32
REL_MAX_DIST = 128
EPS = 1e-6

LANES = 128
SLAB = 256
VMEM_LIMIT_BYTES = 56 * 1024 * 1024

RET_TILE = 512
KV_TILE = 2048
SWA_TILE = 1024
SWA_GROUP = 4

BF16 = jnp.bfloat16
F32 = jnp.float32


def _dot(a, b):
    return jnp.dot(a, b, preferred_element_type=F32)


def _dot_nt(a, b):
    return lax.dot_general(a, b, (((1,), (1,)), ((), ())), preferred_element_type=F32)


def _dot_tn(a, b):
    return lax.dot_general(a, b, (((0,), (0,)), ((), ())), preferred_element_type=F32)


def _dot_cols(a, w_refs):
    return jnp.concatenate([_dot(a, w[...]) for w in w_refs], axis=1)


def _rms(x):
    return x * lax.rsqrt(jnp.mean(x * x, axis=-1, keepdims=True) + EPS)


def _resident(shape):
    nd = len(shape)
    return pl.BlockSpec(shape, lambda *_: (0,) * nd, pipeline_mode=pl.Buffered(1))


def _next_tile_spec(b, n_tiles, t, d):
    last = b * n_tiles - 1
    return pl.BlockSpec((1, t, d), lambda i, j: (jnp.minimum(i * n_tiles + j + 1, last), 0, 0))


def _weight_block(layer, rows, row_block, col_block):
    return pl.BlockSpec((None, rows, SLAB), lambda *_: (layer, row_block, col_block),
                        pipeline_mode=pl.Buffered(1))


RET_IN_BLOCKS = A_IN_WIDTH // SLAB
RET_OUT_BLOCKS = RET_HEADS * (D_MODEL // SLAB)


def _ret_kernel(x_ref, xnext_ref, g_ref, *refs):
    w_in = refs[:RET_IN_BLOCKS]
    w_out = refs[RET_IN_BLOCKS:RET_IN_BLOCKS + RET_OUT_BLOCKS]
    (perm_ref, cos_ref, sin_ref, idec_ref, qdec_ref, kdec_ref, cdec_ref,
     o_ref, state_ref, h_ref, wqk_ref) = refs[RET_IN_BLOCKS + RET_OUT_BLOCKS:]
    qk_blocks = D_MODEL // SLAB
    v_per_head = RET_V_DIM // SLAB
    out_per_head = D_MODEL // SLAB
    assert RET_QK_DIM == SLAB

    def wq(hd):
        return [wqk_ref.at[hd]]

    def wk(hd):
        return [wqk_ref.at[qk_blocks + hd]]

    def wv(hd):
        start = 2 * qk_blocks + hd * v_per_head
        return w_in[start:start + v_per_head]

    def wgate(hd):
        start = 2 * qk_blocks + RET_HEADS * v_per_head + hd * v_per_head
        return w_in[start:start + v_per_head]

    @pl.when(pl.program_id(1) == 0)
    def _():
        state_ref[...] = jnp.zeros_like(state_ref)

    def normed(ref):
        return (_rms(ref[0]) * g_ref[...]).astype(BF16)

    @pl.when((pl.program_id(0) == 0) & (pl.program_id(1) == 0))
    def _():
        h_ref[...] = normed(x_ref)
        for blk in range(2 * qk_blocks):
            wqk_ref[blk] = _dot(w_in[blk][...], perm_ref[...]).astype(BF16)

    def prefetch_h():
        h_ref[...] = normed(xnext_ref)

    x = x_ref[0]
    cos = cos_ref[...]
    sin = sin_ref[...]
    half = RET_QK_DIM // 2
    n_chunks = x.shape[0] // RET_CHUNK

    def rotate(t):
        t1, t2 = t[:, :half], t[:, half:]
        return jnp.concatenate([t1 * cos - t2 * sin, t2 * cos + t1 * sin], axis=1)

    proj = [dict() for _ in range(RET_HEADS)]

    def project_q(hd):
        proj[hd]["q"] = rotate(_dot_cols(h_ref[...], wq(hd))).astype(BF16)

    def project_k(hd):
        k = rotate(_dot_cols(h_ref[...], wk(hd))) * (RET_QK_DIM ** -0.5)
        proj[hd]["k"] = k.astype(BF16)
        kdec = jnp.concatenate([kdec_ref[hd]] * n_chunks, axis=0)
        proj[hd]["kd"] = (k * kdec).astype(BF16)

    def project_v(hd):
        proj[hd]["v"] = _dot_cols(h_ref[...], wv(hd)).astype(BF16)

    def project_gate(hd):
        proj[hd]["gate"] = _dot_cols(h_ref[...], wgate(hd))

    def finish(hd, outs, acc):
        o = _rms(jnp.concatenate(outs, axis=0))
        gate = proj[hd]["gate"]
        y = (o * (gate * jax.nn.sigmoid(gate))).astype(BF16)
        return acc + _dot_cols(y, w_out[hd * out_per_head:(hd + 1) * out_per_head])

    project_q(0)
    project_k(0)
    project_v(0)
    acc = x
    pending = None
    for hd in range(RET_HEADS):
        fillers = [functools.partial(project_gate, hd)]
        if hd + 1 < RET_HEADS:
            fillers += [functools.partial(f, hd + 1) for f in (project_q, project_k, project_v)]
        else:
            fillers += [prefetch_h]
        qb, kb, kd, v = proj[hd]["q"], proj[hd]["k"], proj[hd]["kd"], proj[hd]["v"]
        st = state_ref[hd]
        outs = []
        for c in range(n_chunks):
            rows = slice(c * RET_CHUNK, (c + 1) * RET_CHUNK)
            qc, kc, vc = qb[rows], kb[rows], v[rows]
            scores = _dot_nt(qc, kc) * idec_ref[hd]
            inter = _dot(qc, st.astype(BF16)) * qdec_ref[hd]
            st = st * cdec_ref[hd] + _dot_tn(kd[rows], vc)
            lo_f = (c * len(fillers)) // n_chunks
            hi_f = ((c + 1) * len(fillers)) // n_chunks
            for f in fillers[lo_f:hi_f]:
                f()
            if c == 0 and pending is not None:
                acc = finish(*pending, acc)
            outs.append(_dot(scores.astype(BF16), vc) + inter)
        state_ref[hd] = st
        pending = (hd, outs)
    o_ref[0] = finish(*pending, acc)


def _retention_layer(x, g, w_in, w_out, layer, cos, sin, idec, qdec, kdec, cdec):
    b, s, d = x.shape
    t = RET_TILE
    in_blocks = [_weight_block(layer, d, 0, c) for c in range(RET_IN_BLOCKS)]
    out_blocks = [_weight_block(layer, RET_V_DIM, hd, c)
                  for hd in range(RET_HEADS) for c in range(d // SLAB)]
    n_tiles = s // t
    half = RET_QK_DIM // 2
    src = jnp.arange(RET_QK_DIM)
    dst = jnp.where(src % 2 == 0, src // 2, half + src // 2)
    perm = (dst[:, None] == jnp.arange(RET_QK_DIM)[None, :]).astype(BF16)
    return pl.pallas_call(
        _ret_kernel,
        grid=(b, n_tiles),
        in_specs=[
            pl.BlockSpec((1, t, d), lambda i, j: (i, j, 0)),
            _next_tile_spec(b, n_tiles, t, d),
            _resident((1, d)),
            *in_blocks, *out_blocks,
            _resident(perm.shape),
            pl.BlockSpec((t, half), lambda i, j: (j, 0)),
            pl.BlockSpec((t, half), lambda i, j: (j, 0)),
            _resident(idec.shape), _resident(qdec.shape), _resident(kdec.shape),
            _resident(cdec.shape),
        ],
        out_specs=pl.BlockSpec((1, t, d), lambda i, j: (i, j, 0)),
        out_shape=jax.ShapeDtypeStruct(x.shape, x.dtype),
        scratch_shapes=[pltpu.VMEM((RET_HEADS, RET_QK_DIM, RET_V_DIM), F32),
                        pltpu.VMEM((t, d), BF16),
                        pltpu.VMEM((2 * d // SLAB, d, SLAB), BF16)],
        compiler_params=pltpu.CompilerParams(
            dimension_semantics=("arbitrary", "arbitrary"),
            vmem_limit_bytes=VMEM_LIMIT_BYTES),
        name="retention_layer",
    )(x, x.reshape(b * n_tiles, t, d), g.reshape(1, d),
      *([w_in] * RET_IN_BLOCKS), *([w_out] * RET_OUT_BLOCKS),
      perm, cos, sin, idec, qdec, kdec, cdec)


def _retention_tables(s):
    dk = RET_QK_DIM
    angle = 1.0 / (10000.0 ** jnp.linspace(0.0, 1.0, dk // 2, dtype=F32))
    pos = jnp.arange(s, dtype=F32)[:, None]
    sin, cos = jnp.sin(pos * angle), jnp.cos(pos * angle)
    c = RET_CHUNK
    log_gamma = jnp.log(1.0 - 2.0 ** (-5.0 - jnp.arange(RET_HEADS, dtype=F32)))
    idx = jnp.arange(c, dtype=F32)
    diff = idx[:, None] - idx[None, :]
    idec = jnp.where(diff[None] >= 0,
                     jnp.exp(jnp.maximum(diff, 0.0)[None] * log_gamma[:, None, None]), 0.0)
    qdec = jnp.exp((idx + 1.0)[None, :, None] * log_gamma[:, None, None])
    kdec = jnp.exp((c - 1.0 - idx)[None, :, None] * log_gamma[:, None, None])
    cdec = jnp.exp(c * log_gamma)[:, None, None]
    return cos, sin, idec.astype(F32), qdec, kdec, cdec


def _kv_kernel(x_ref, g_ref, w_ref, gk_ref, k_ref, vt_ref):
    h = (_rms(x_ref[0]) * g_ref[...]).astype(BF16)
    kv = _dot(h, w_ref[...])
    k, v = kv[:, :KV_WIDTH], kv[:, KV_WIDTH:]
    lo = lax.broadcasted_iota(jnp.int32, (1, KV_WIDTH), 1) < SWA_HEAD_DIM
    sq = k * k
    ms_lo = jnp.sum(jnp.where(lo, sq, 0.0), axis=-1, keepdims=True) / SWA_HEAD_DIM
    ms_hi = jnp.sum(jnp.where(lo, 0.0, sq), axis=-1, keepdims=True) / SWA_HEAD_DIM
    r = jnp.where(lo, lax.rsqrt(ms_lo + EPS), lax.rsqrt(ms_hi + EPS))
    k_ref[0] = k * r * gk_ref[...]
    vt_ref[0] = v.T


def _shared_kv(x, g, w_kv, g_k):
    b, s, d = x.shape
    t = KV_TILE
    return pl.pallas_call(
        _kv_kernel,
        grid=(b, s // t),
        in_specs=[
            pl.BlockSpec((1, t, d), lambda i, j: (i, j, 0)),
            _resident((1, d)),
            _resident(w_kv.shape),
            _resident((1, KV_WIDTH)),
        ],
        out_specs=[pl.BlockSpec((1, t, KV_WIDTH), lambda i, j: (i, j, 0)),
                   pl.BlockSpec((1, KV_WIDTH, t), lambda i, j: (i, 0, j))],
        out_shape=[jax.ShapeDtypeStruct((b, s, KV_WIDTH), F32),
                   jax.ShapeDtypeStruct((b, KV_WIDTH, s), F32)],
        compiler_params=pltpu.CompilerParams(
            dimension_semantics=("parallel", "parallel"),
            vmem_limit_bytes=VMEM_LIMIT_BYTES),
        name="shared_kv",
    )(x, g.reshape(1, d), w_kv.astype(BF16), jnp.tile(g_k, SWA_KV_HEADS).reshape(1, KV_WIDTH))


def _prev_key_valid():
    j = lax.broadcasted_iota(jnp.int32, (BLOCK, BLOCK), 0)
    i = lax.broadcasted_iota(jnp.int32, (BLOCK, BLOCK), 1)
    return j > i, j, i


def _bias_kernel(rel_ref, o_ref):
    hd = pl.program_id(0)
    prev, j, i = _prev_key_valid()
    dist = jnp.where(prev, i + BLOCK - j, i - j)
    max_exact = REL_BUCKETS // 2
    dist_f = jnp.maximum(dist, 1).astype(F32)
    large = max_exact + jnp.floor(jnp.log(dist_f / max_exact) / math.log(REL_MAX_DIST / max_exact)
                                  * (REL_BUCKETS - max_exact)).astype(jnp.int32)
    large = jnp.minimum(large, REL_BUCKETS - 1)
    bucket = jnp.where(dist < max_exact, dist, large)
    bias = jnp.zeros((BLOCK, BLOCK), F32)
    for bk in range(REL_BUCKETS):
        bias = jnp.where(bucket == bk, rel_ref[bk, hd], bias)
    o_ref[0, 0] = bias
    o_ref[1, 0] = jnp.where(prev, jnp.float32(-jnp.inf), bias)


def _bias_tables(rel_bias):
    return pl.pallas_call(
        _bias_kernel,
        grid=(SWA_Q_HEADS,),
        in_specs=[pl.BlockSpec(memory_space=pltpu.SMEM)],
        out_specs=pl.BlockSpec((2, 1, BLOCK, BLOCK), lambda h: (0, h, 0, 0)),
        out_shape=jax.ShapeDtypeStruct((2, SWA_Q_HEADS, BLOCK, BLOCK), F32),
        name="rel_bias_tables",
    )(rel_bias)


SWA_GROUPS = SWA_Q_HEADS // SWA_GROUP
SWA_IN_BLOCKS = 2 * SWA_Q_WIDTH // SLAB
SWA_OUT_BLOCKS = SWA_GROUPS * (D_MODEL // SLAB)
assert SWA_GROUP * SWA_HEAD_DIM == SLAB


def _swa_kernel(sinks_ref, x_ref, xnext_ref, g_ref, gq_ref, *refs):
    w_in = refs[:SWA_IN_BLOCKS]
    w_out = refs[SWA_IN_BLOCKS:SWA_IN_BLOCKS + SWA_OUT_BLOCKS]
    (kc_ref, kp_ref, vtc_ref, vtp_ref, bias_ref,
     o_ref, h_ref, q0_ref) = refs[SWA_IN_BLOCKS + SWA_OUT_BLOCKS:]
    out_per_group = D_MODEL // SLAB

    def normed(ref):
        return (_rms(ref[0]) * g_ref[...]).astype(BF16)

    @pl.when((pl.program_id(0) == 0) & (pl.program_id(1) == 0))
    def _():
        h_ref[...] = normed(x_ref)
        q0_ref[...] = _dot(h_ref[...], w_in[0][...])

    def prefetch_h():
        h_ref[...] = normed(xnext_ref)

    def prefetch_q0():
        q0_ref[...] = _dot(h_ref[...], w_in[0][...])

    first = (pl.program_id(1) == 0).astype(jnp.int32)
    x = x_ref[0]
    n_blocks = x.shape[0] // BLOCK
    n_groups = SWA_GROUPS

    lo = lax.broadcasted_iota(jnp.int32, (1, LANES), 1) < SWA_HEAD_DIM
    lo_rows = lax.broadcasted_iota(jnp.int32, (LANES, 1), 0) < SWA_HEAD_DIM

    kall = jnp.concatenate([kp_ref[0], kc_ref[0]], axis=0)
    kswap = pltpu.roll(kall, SWA_HEAD_DIM, 1)
    kdup = [jnp.where(lo, kall, kswap).astype(BF16), jnp.where(lo, kswap, kall).astype(BF16)]
    vt = jnp.concatenate([vtp_ref[0], vtc_ref[0]], axis=1).astype(BF16)
    gq = gq_ref[...]
    prev_valid, _, _ = _prev_key_valid()

    heads_per_kv = SWA_Q_HEADS // SWA_KV_HEADS

    q_slab = [None] * n_groups
    gate_slab = [None] * n_groups
    y_blocks = [[None] * n_blocks for _ in range(n_groups)]

    def project_q(grp):
        q_slab[grp] = _dot(h_ref[...], w_in[grp][...])

    def project_gate(grp):
        gate_slab[grp] = _dot(h_ref[...], w_in[n_groups + grp][...])

    def scores(blk, grp):
        band = slice(blk * BLOCK, (blk + 2) * BLOCK)
        kv = (SWA_GROUP * grp) // heads_per_kv
        tiles = []
        for pair in range(2):
            qp = q_slab[grp][blk * BLOCK:(blk + 1) * BLOCK, pair * LANES:(pair + 1) * LANES]
            sq = qp * qp
            ms_lo = jnp.sum(jnp.where(lo, sq, 0.0), axis=-1, keepdims=True) / SWA_HEAD_DIM
            ms_hi = jnp.sum(jnp.where(lo, 0.0, sq), axis=-1, keepdims=True) / SWA_HEAD_DIM
            r = jnp.where(lo, lax.rsqrt(ms_lo + EPS), lax.rsqrt(ms_hi + EPS))
            qn = qp * r * gq
            qt = qn.T
            qstack = jnp.concatenate([jnp.where(lo_rows, qt, 0.0), jnp.where(lo_rows, 0.0, qt)],
                                     axis=1).astype(BF16)
            tiles.append(_dot(kdup[kv][band], qstack))
        return tiles

    def softmax(blk, grp, tiles):
        table = first if blk == 0 else 0
        pairs = []
        for pair in range(2):
            heads = []
            for pos in range(2):
                hd = SWA_GROUP * grp + 2 * pair + pos
                sink = sinks_ref[hd]
                cols = slice(pos * BLOCK, (pos + 1) * BLOCK)
                st = tiles[pair]
                s = (jnp.where(prev_valid, st[:BLOCK, cols], st[BLOCK:, cols])
                     + bias_ref[table, hd])
                m = jnp.maximum(jnp.max(s, axis=0, keepdims=True), sink)
                e = jnp.exp(s - m)
                den = jnp.sum(e, axis=0, keepdims=True) + jnp.exp(sink - m)
                pn = e * (1.0 / den)
                heads.append(jnp.concatenate([jnp.where(prev_valid, pn, 0.0),
                                              jnp.where(prev_valid, 0.0, pn)],
                                             axis=0).astype(BF16))
            pairs.append(jnp.concatenate(heads, axis=1))
        return pairs

    def finish(blk, grp, pairs):
        band = slice(blk * BLOCK, (blk + 2) * BLOCK)
        kv = (SWA_GROUP * grp) // heads_per_kv
        vt_band = vt[kv * SWA_HEAD_DIM:(kv + 1) * SWA_HEAD_DIM, band]
        ot = [_dot(vt_band, pairs[pair]) for pair in range(2)]
        ot = jnp.concatenate([ot[0][:, :BLOCK], ot[0][:, BLOCK:],
                              ot[1][:, :BLOCK], ot[1][:, BLOCK:]], axis=0)
        o = ot.T
        gate = gate_slab[grp][blk * BLOCK:(blk + 1) * BLOCK]
        y_blocks[grp][blk] = (o * (gate * jax.nn.sigmoid(gate))).astype(BF16)

    def out_project(grp):
        y = jnp.concatenate(y_blocks[grp], axis=0)
        part = _dot_cols(y, w_out[grp * out_per_group:(grp + 1) * out_per_group])
        if grp == 0:
            o_ref[0] = x + part
        else:
            o_ref[0] += part

    items = [(blk, grp) for grp in range(n_groups) for blk in range(n_blocks)]
    last = len(items) - 1
    jobs = []
    for grp in range(n_groups):
        jobs.append((functools.partial(project_gate, grp), 0, grp * n_blocks + 1))
        if grp + 1 < n_groups:
            jobs.append((functools.partial(project_q, grp + 1), 0, (grp + 1) * n_blocks - 2))
        if grp >= 1:
            jobs.append((functools.partial(out_project, grp - 1), grp * n_blocks + 1, last))
    jobs += [(prefetch_h, 0, last), (prefetch_q0, n_blocks - 1, last)]
    pairs = [jobs[n:n + 2] for n in range(0, len(jobs), 2)]
    fillers = {}
    slot = -1
    for n, pair in enumerate(pairs):
        slot = max(slot + 1, (n * len(items)) // len(pairs), *(first for _, first, _ in pair))
        assert all(slot <= last_item for _, _, last_item in pair), "filler after its consumer"
        fillers[slot] = [job for job, _, _ in pair]
    q_slab[0] = q0_ref
    pending = None
    for job in fillers.pop(0, []):
        job()
    tiles = scores(*items[0])
    for idx, (blk, grp) in enumerate(items):
        next_tiles = scores(*items[idx + 1]) if idx + 1 < len(items) else None
        for job in fillers.get(idx, []):
            job()
        if pending is not None:
            finish(*pending)
        pending = (blk, grp, softmax(blk, grp, tiles))
        tiles = next_tiles
    finish(*pending)
    out_project(n_groups - 1)


def _swa_layer(x, g, w_in, g_q, sinks, w_out, layer, k, vt, bias):
    b, s, d = x.shape
    t = SWA_TILE
    in_blocks = [_weight_block(layer, d, 0, c) for c in range(SWA_IN_BLOCKS)]
    out_blocks = [_weight_block(layer, SLAB, grp, c)
                  for grp in range(SWA_GROUPS) for c in range(d // SLAB)]

    def prev_block(j):
        return jnp.maximum(j * (t // BLOCK) - 1, 0)

    k_cur = pl.BlockSpec((1, t, KV_WIDTH), lambda i, j: (i, j, 0))
    k_prev = pl.BlockSpec((1, BLOCK, KV_WIDTH), lambda i, j: (i, prev_block(j), 0))
    vt_cur = pl.BlockSpec((1, KV_WIDTH, t), lambda i, j: (i, 0, j))
    vt_prev = pl.BlockSpec((1, KV_WIDTH, BLOCK), lambda i, j: (i, 0, prev_block(j)))
    n_tiles = s // t
    return pl.pallas_call(
        _swa_kernel,
        grid=(b, n_tiles),
        in_specs=[
            pl.BlockSpec(memory_space=pltpu.SMEM),
            pl.BlockSpec((1, t, d), lambda i, j: (i, j, 0)),
            _next_tile_spec(b, n_tiles, t, d),
            _resident((1, d)),
            _resident((1, LANES)),
            *in_blocks, *out_blocks,
            k_cur, k_prev, vt_cur, vt_prev,
            _resident(bias.shape),
        ],
        out_specs=pl.BlockSpec((1, t, d), lambda i, j: (i, j, 0)),
        out_shape=jax.ShapeDtypeStruct(x.shape, x.dtype),
        scratch_shapes=[pltpu.VMEM((t, d), BF16), pltpu.VMEM((t, SLAB), F32)],
        compiler_params=pltpu.CompilerParams(
            dimension_semantics=("arbitrary", "arbitrary"),
            vmem_limit_bytes=VMEM_LIMIT_BYTES),
        name="swa_layer",
    )(sinks, x, x.reshape(b * n_tiles, t, d), g.reshape(1, d),
      jnp.tile(g_q, 2).reshape(1, LANES) * (SWA_HEAD_DIM ** -0.5),
      *([w_in] * SWA_IN_BLOCKS), *([w_out] * SWA_OUT_BLOCKS), k, k, vt, vt, bias)


def kernel(x, a_norm_g, a_w_in, a_w_out, kv_norm_g, w_kv, k_norm_g, rel_bias,
           b_norm_g, b_w_in, b_q_norm_g, b_sinks, b_w_out):
    s = x.shape[1]
    tables = _retention_tables(s)
    a_w_in, a_w_out = a_w_in.astype(BF16), a_w_out.astype(BF16)
    b_w_in, b_w_out = b_w_in.astype(BF16), b_w_out.astype(BF16)
    for layer in range(a_w_in.shape[0]):
        x = _retention_layer(x, a_norm_g[layer], a_w_in, a_w_out, layer, *tables)
    k, vt = _shared_kv(x, kv_norm_g, w_kv, k_norm_g)
    bias = _bias_tables(rel_bias)
    for layer in range(b_w_in.shape[0]):
        x = _swa_layer(x, b_norm_g[layer], b_w_in, b_q_norm_g[layer], b_sinks[layer], b_w_out,
                       layer, k, vt, bias)
    return x
```

```python
import functools
import math

import jax
import jax.numpy as jnp
from jax import lax
from jax.experimental import pallas as pl
from jax.experimental.pallas import tpu as pltpu

D_MODEL = 1024
RET_HEADS = 4
RET_QK_DIM = D_MODEL // RET_HEADS
RET_V_DIM = 2 * RET_QK_DIM
RET_V_TOTAL = RET_HEADS * RET_V_DIM
RET_CHUNK = 128
A_IN_WIDTH = 2 * D_MODEL + 2 * RET_V_TOTAL
SWA_HEAD_DIM = 64
SWA_Q_HEADS = D_MODEL // SWA_HEAD_DIM
SWA_KV_HEADS = SWA_Q_HEADS // 8
SWA_Q_WIDTH = SWA_Q_HEADS * SWA_HEAD_DIM
KV_WIDTH = SWA_KV_HEADS * SWA_HEAD_DIM
WINDOW = 128
BLOCK = 128
REL_BUCKETS = 32
REL_MAX_DIST = 128
EPS = 1e-6

LANES = 128
SLAB = 256
VMEM_LIMIT_BYTES = 56 * 1024 * 1024

RET_TILE = 512
KV_TILE = 2048
SWA_TILE = 1024
SWA_GROUP = 4

BF16 = jnp.bfloat16
F32 = jnp.float32


def _dot(a, b):
    return jnp.dot(a, b, preferred_element_type=F32)


def _dot_nt(a, b):
    return lax.dot_general(a, b, (((1,), (1,)), ((), ())), preferred_element_type=F32)


def _dot_tn(a, b):
    return lax.dot_general(a, b, (((0,), (0,)), ((), ())), preferred_element_type=F32)


def _dot_cols(a, w_refs):
    return jnp.concatenate([_dot(a, w[...]) for w in w_refs], axis=1)


def _rms(x):
    return x * lax.rsqrt(jnp.mean(x * x, axis=-1, keepdims=True) + EPS)


def _resident(shape):
    nd = len(shape)
    return pl.BlockSpec(shape, lambda *_: (0,) * nd, pipeline_mode=pl.Buffered(1))


def _next_tile_spec(b, n_tiles, t, d):
    last = b * n_tiles - 1
    return pl.BlockSpec((1, t, d), lambda i, j: (jnp.minimum(i * n_tiles + j + 1, last), 0, 0))


def _weight_block(layer, rows, row_block, col_block):
    return pl.BlockSpec((None, rows, SLAB), lambda *_: (layer, row_block, col_block),
                        pipeline_mode=pl.Buffered(1))


RET_IN_BLOCKS = A_IN_WIDTH // SLAB
RET_OUT_BLOCKS = RET_HEADS * (D_MODEL // SLAB)


def _ret_kernel(x_ref, xnext_ref, g_ref, *refs):
    w_in = refs[:RET_IN_BLOCKS]
    w_out = refs[RET_IN_BLOCKS:RET_IN_BLOCKS + RET_OUT_BLOCKS]
    (perm_ref, cos_ref, sin_ref, idec_ref, qdec_ref, kdec_ref, cdec_ref,
     o_ref, state_ref, h_ref, wqk_ref) = refs[RET_IN_BLOCKS + RET_OUT_BLOCKS:]
    qk_blocks = D_MODEL // SLAB
    v_per_head = RET_V_DIM // SLAB
    out_per_head = D_MODEL // SLAB
    assert RET_QK_DIM == SLAB

    def wq(hd):
        return [wqk_ref.at[hd]]

    def wk(hd):
        return [wqk_ref.at[qk_blocks + hd]]

    def wv(hd):
        start = 2 * qk_blocks + hd * v_per_head
        return w_in[start:start + v_per_head]

    def wgate(hd):
        start = 2 * qk_blocks + RET_HEADS * v_per_head + hd * v_per_head
        return w_in[start:start + v_per_head]

    @pl.when(pl.program_id(1) == 0)
    def _():
        state_ref[...] = jnp.zeros_like(state_ref)

    def normed(ref):
        return (_rms(ref[0]) * g_ref[...]).astype(BF16)

    @pl.when((pl.program_id(0) == 0) & (pl.program_id(1) == 0))
    def _():
        h_ref[...] = normed(x_ref)
        for blk in range(2 * qk_blocks):
            wqk_ref[blk] = _dot(w_in[blk][...], perm_ref[...]).astype(BF16)

    def prefetch_h():
        h_ref[...] = normed(xnext_ref)

    x = x_ref[0]
    cos = cos_ref[...]
    sin = sin_ref[...]
    half = RET_QK_DIM // 2
    n_chunks = x.shape[0] // RET_CHUNK

    def rotate(t):
        t1, t2 = t[:, :half], t[:, half:]
        return jnp.concatenate([t1 * cos - t2 * sin, t2 * cos + t1 * sin], axis=1)

    proj = [dict() for _ in range(RET_HEADS)]

    def project_q(hd):
        proj[hd]["q"] = rotate(_dot_cols(h_ref[...], wq(hd))).astype(BF16)

    def project_k(hd):
        k = rotate(_dot_cols(h_ref[...], wk(hd))) * (RET_QK_DIM ** -0.5)
        proj[hd]["k"] = k.astype(BF16)
        kdec = jnp.concatenate([kdec_ref[hd]] * n_chunks, axis=0)
        proj[hd]["kd"] = (k * kdec).astype(BF16)

    def project_v(hd):
        proj[hd]["v"] = _dot_cols(h_ref[...], wv(hd)).astype(BF16)

    def project_gate(hd):
        proj[hd]["gate"] = _dot_cols(h_ref[...], wgate(hd))

    def finish(hd, outs, acc):
        o = _rms(jnp.concatenate(outs, axis=0))
        gate = proj[hd]["gate"]
        y = (o * (gate * jax.nn.sigmoid(gate))).astype(BF16)
        return acc + _dot_cols(y, w_out[hd * out_per_head:(hd + 1) * out_per_head])

    project_q(0)
    project_k(0)
    project_v(0)
    acc = x
    pending = None
    for hd in range(RET_HEADS):
        fillers = [functools.partial(project_gate, hd)]
        if hd + 1 < RET_HEADS:
            fillers += [functools.partial(f, hd + 1) for f in (project_q, project_k, project_v)]
        else:
            fillers += [prefetch_h]
        qb, kb, kd, v = proj[hd]["q"], proj[hd]["k"], proj[hd]["kd"], proj[hd]["v"]
        st = state_ref[hd]
        outs = []
        for c in range(n_chunks):
            rows = slice(c * RET_CHUNK, (c + 1) * RET_CHUNK)
            qc, kc, vc = qb[rows], kb[rows], v[rows]
            scores = _dot_nt(qc, kc) * idec_ref[hd]
            inter = _dot(qc, st.astype(BF16)) * qdec_ref[hd]
            st = st * cdec_ref[hd] + _dot_tn(kd[rows], vc)
            lo_f = (c * len(fillers)) // n_chunks
            hi_f = ((c + 1) * len(fillers)) // n_chunks
            for f in fillers[lo_f:hi_f]:
                f()
            if c == 0 and pending is not None:
                acc = finish(*pending, acc)
            outs.append(_dot(scores.astype(BF16), vc) + inter)
        state_ref[hd] = st
        pending = (hd, outs)
    o_ref[0] = finish(*pending, acc)


def _retention_layer(x, g, w_in, w_out, layer, cos, sin, idec, qdec, kdec, cdec):
    b, s, d = x.shape
    t = RET_TILE
    in_blocks = [_weight_block(layer, d, 0, c) for c in range(RET_IN_BLOCKS)]
    out_blocks = [_weight_block(layer, RET_V_DIM, hd, c)
                  for hd in range(RET_HEADS) for c in range(d // SLAB)]
    n_tiles = s // t
    half = RET_QK_DIM // 2
    src = jnp.arange(RET_QK_DIM)
    dst = jnp.where(src % 2 == 0, src // 2, half + src // 2)
    perm = (dst[:, None] == jnp.arange(RET_QK_DIM)[None, :]).astype(BF16)
    return pl.pallas_call(
        _ret_kernel,
        grid=(b, n_tiles),
        in_specs=[
            pl.BlockSpec((1, t, d), lambda i, j: (i, j, 0)),
            _next_tile_spec(b, n_tiles, t, d),
            _resident((1, d)),
            *in_blocks, *out_blocks,
            _resident(perm.shape),
            pl.BlockSpec((t, half), lambda i, j: (j, 0)),
            pl.BlockSpec((t, half), lambda i, j: (j, 0)),
            _resident(idec.shape), _resident(qdec.shape), _resident(kdec.shape),
            _resident(cdec.shape),
        ],
        out_specs=pl.BlockSpec((1, t, d), lambda i, j: (i, j, 0)),
        out_shape=jax.ShapeDtypeStruct(x.shape, x.dtype),
        scratch_shapes=[pltpu.VMEM((RET_HEADS, RET_QK_DIM, RET_V_DIM), F32),
                        pltpu.VMEM((t, d), BF16),
                        pltpu.VMEM((2 * d // SLAB, d, SLAB), BF16)],
        compiler_params=pltpu.CompilerParams(
            dimension_semantics=("arbitrary", "arbitrary"),
            vmem_limit_bytes=VMEM_LIMIT_BYTES),
        name="retention_layer",
    )(x, x.reshape(b * n_tiles, t, d), g.reshape(1, d),
      *([w_in] * RET_IN_BLOCKS), *([w_out] * RET_OUT_BLOCKS),
      perm, cos, sin, idec, qdec, kdec, cdec)


def _retention_tables(s):
    dk = RET_QK_DIM
    angle = 1.0 / (10000.0 ** jnp.linspace(0.0, 1.0, dk // 2, dtype=F32))
    pos = jnp.arange(s, dtype=F32)[:, None]
    sin, cos = jnp.sin(pos * angle), jnp.cos(pos * angle)
    c = RET_CHUNK
    log_gamma = jnp.log(1.0 - 2.0 ** (-5.0 - jnp.arange(RET_HEADS, dtype=F32)))
    idx = jnp.arange(c, dtype=F32)
    diff = idx[:, None] - idx[None, :]
    idec = jnp.where(diff[None] >= 0,
                     jnp.exp(jnp.maximum(diff, 0.0)[None] * log_gamma[:, None, None]), 0.0)
    qdec = jnp.exp((idx + 1.0)[None, :, None] * log_gamma[:, None, None])
    kdec = jnp.exp((c - 1.0 - idx)[None, :, None] * log_gamma[:, None, None])
    cdec = jnp.exp(c * log_gamma)[:, None, None]
    return cos, sin, idec.astype(F32), qdec, kdec, cdec


def _kv_kernel(x_ref, g_ref, w_ref, gk_ref, k_ref, vt_ref):
    h = (_rms(x_ref[0]) * g_ref[...]).astype(BF16)
    kv = _dot(h, w_ref[...])
    k, v = kv[:, :KV_WIDTH], kv[:, KV_WIDTH:]
    lo = lax.broadcasted_iota(jnp.int32, (1, KV_WIDTH), 1) < SWA_HEAD_DIM
    sq = k * k
    ms_lo = jnp.sum(jnp.where(lo, sq, 0.0), axis=-1, keepdims=True) / SWA_HEAD_DIM
    ms_hi = jnp.sum(jnp.where(lo, 0.0, sq), axis=-1, keepdims=True) / SWA_HEAD_DIM
    r = jnp.where(lo, lax.rsqrt(ms_lo + EPS), lax.rsqrt(ms_hi + EPS))
    k_ref[0] = k * r * gk_ref[...]
    vt_ref[0] = v.T


def _shared_kv(x, g, w_kv, g_k):
    b, s, d = x.shape
    t = KV_TILE
    return pl.pallas_call(
        _kv_kernel,
        grid=(b, s // t),
        in_specs=[
            pl.BlockSpec((1, t, d), lambda i, j: (i, j, 0)),
            _resident((1, d)),
            _resident(w_kv.shape),
            _resident((1, KV_WIDTH)),
        ],
        out_specs=[pl.BlockSpec((1, t, KV_WIDTH), lambda i, j: (i, j, 0)),
                   pl.BlockSpec((1, KV_WIDTH, t), lambda i, j: (i, 0, j))],
        out_shape=[jax.ShapeDtypeStruct((b, s, KV_WIDTH), F32),
                   jax.ShapeDtypeStruct((b, KV_WIDTH, s), F32)],
        compiler_params=pltpu.CompilerParams(
            dimension_semantics=("parallel", "parallel"),
            vmem_limit_bytes=VMEM_LIMIT_BYTES),
        name="shared_kv",
    )(x, g.reshape(1, d), w_kv.astype(BF16), jnp.tile(g_k, SWA_KV_HEADS).reshape(1, KV_WIDTH))


def _prev_key_valid():
    j = lax.broadcasted_iota(jnp.int32, (BLOCK, BLOCK), 0)
    i = lax.broadcasted_iota(jnp.int32, (BLOCK, BLOCK), 1)
    return j > i, j, i


def _bias_kernel(rel_ref, o_ref):
    hd = pl.program_id(0)
    prev, j, i = _prev_key_valid()
    dist = jnp.where(prev, i + BLOCK - j, i - j)
    max_exact = REL_BUCKETS // 2
    dist_f = jnp.maximum(dist, 1).astype(F32)
    large = max_exact + jnp.floor(jnp.log(dist_f / max_exact) / math.log(REL_MAX_DIST / max_exact)
                                  * (REL_BUCKETS - max_exact)).astype(jnp.int32)
    large = jnp.minimum(large, REL_BUCKETS - 1)
    bucket = jnp.where(dist < max_exact, dist, large)
    bias = jnp.zeros((BLOCK, BLOCK), F32)
    for bk in range(REL_BUCKETS):
        bias = jnp.where(bucket == bk, rel_ref[bk, hd], bias)
    o_ref[0, 0] = bias
    o_ref[1, 0] = jnp.where(prev, jnp.float32(-jnp.inf), bias)


def _bias_tables(rel_bias):
    return pl.pallas_call(
        _bias_kernel,
        grid=(SWA_Q_HEADS,),
        in_specs=[pl.BlockSpec(memory_space=pltpu.SMEM)],
        out_specs=pl.BlockSpec((2, 1, BLOCK, BLOCK), lambda h: (0, h, 0, 0)),
        out_shape=jax.ShapeDtypeStruct((2, SWA_Q_HEADS, BLOCK, BLOCK), F32),
        name="rel_bias_tables",
    )(rel_bias)


SWA_GROUPS = SWA_Q_HEADS // SWA_GROUP
SWA_IN_BLOCKS = 2 * SWA_Q_WIDTH // SLAB
SWA_OUT_BLOCKS = SWA_GROUPS * (D_MODEL // SLAB)
assert SWA_GROUP * SWA_HEAD_DIM == SLAB


def _swa_kernel(sinks_ref, x_ref, xnext_ref, g_ref, gq_ref, *refs):
    w_in = refs[:SWA_IN_BLOCKS]
    w_out = refs[SWA_IN_BLOCKS:SWA_IN_BLOCKS + SWA_OUT_BLOCKS]
    (kc_ref, kp_ref, vtc_ref, vtp_ref, bias_ref,
     o_ref, h_ref, q0_ref) = refs[SWA_IN_BLOCKS + SWA_OUT_BLOCKS:]
    out_per_group = D_MODEL // SLAB

    def normed(ref):
        return (_rms(ref[0]) * g_ref[...]).astype(BF16)

    @pl.when((pl.program_id(0) == 0) & (pl.program_id(1) == 0))
    def _():
        h_ref[...] = normed(x_ref)
        q0_ref[...] = _dot(h_ref[...], w_in[0][...])

    def prefetch_h():
        h_ref[...] = normed(xnext_ref)

    def prefetch_q0():
        q0_ref[...] = _dot(h_ref[...], w_in[0][...])

    first = (pl.program_id(1) == 0).astype(jnp.int32)
    x = x_ref[0]
    n_blocks = x.shape[0] // BLOCK
    n_groups = SWA_GROUPS

    lo = lax.broadcasted_iota(jnp.int32, (1, LANES), 1) < SWA_HEAD_DIM
    lo_rows = lax.broadcasted_iota(jnp.int32, (LANES, 1), 0) < SWA_HEAD_DIM

    kall = jnp.concatenate([kp_ref[0], kc_ref[0]], axis=0)
    kswap = pltpu.roll(kall, SWA_HEAD_DIM, 1)
    kdup = [jnp.where(lo, kall, kswap).astype(BF16), jnp.where(lo, kswap, kall).astype(BF16)]
    vt = jnp.concatenate([vtp_ref[0], vtc_ref[0]], axis=1).astype(BF16)
    gq = gq_ref[...]
    prev_valid, _, _ = _prev_key_valid()

    heads_per_kv = SWA_Q_HEADS // SWA_KV_HEADS

    q_slab = [None] * n_groups
    gate_slab = [None] * n_groups
    y_blocks = [[None] * n_blocks for _ in range(n_groups)]

    def project_q(grp):
        q_slab[grp] = _dot(h_ref[...], w_in[grp][...])

    def project_gate(grp):
        gate_slab[grp] = _dot(h_ref[...], w_in[n_groups + grp][...])

    def scores(blk, grp):
        band = slice(blk * BLOCK, (blk + 2) * BLOCK)
        kv = (SWA_GROUP * grp) // heads_per_kv
        tiles = []
        for pair in range(2):
            qp = q_slab[grp][blk * BLOCK:(blk + 1) * BLOCK, pair * LANES:(pair + 1) * LANES]
            qt = qp.T
            sq = qt * qt
            ms_lo = jnp.sum(sq[:SWA_HEAD_DIM], axis=0, keepdims=True) / SWA_HEAD_DIM
            ms_hi = jnp.sum(sq[SWA_HEAD_DIM:], axis=0, keepdims=True) / SWA_HEAD_DIM
            r = jnp.where(lo_rows, lax.rsqrt(ms_lo + EPS), lax.rsqrt(ms_hi + EPS))
            qt = qt * r * gq
            qstack = jnp.concatenate([jnp.where(lo_rows, qt, 0.0), jnp.where(lo_rows, 0.0, qt)],
                                     axis=1).astype(BF16)
            tiles.append(_dot(kdup[kv][band], qstack))
        return tiles

    def softmax(blk, grp, tiles):
        table = first if blk == 0 else 0
        pairs = []
        for pair in range(2):
            heads = []
            for pos in range(2):
                hd = SWA_GROUP * grp + 2 * pair + pos
                sink = sinks_ref[hd]
                cols = slice(pos * BLOCK, (pos + 1) * BLOCK)
                st = tiles[pair]
                s = (jnp.where(prev_valid, st[:BLOCK, cols], st[BLOCK:, cols])
                     + bias_ref[table, hd])
                m = jnp.maximum(jnp.max(s, axis=0, keepdims=True), sink)
                e = jnp.exp(s - m)
                den = jnp.sum(e, axis=0, keepdims=True) + jnp.exp(sink - m)
                pn = e * (1.0 / den)
                heads.append(jnp.concatenate([jnp.where(prev_valid, pn, 0.0),
                                              jnp.where(prev_valid, 0.0, pn)],
                                             axis=0).astype(BF16))
            pairs.append(jnp.concatenate(heads, axis=1))
        return pairs

    def finish(blk, grp, pairs):
        band = slice(blk * BLOCK, (blk + 2) * BLOCK)
        kv = (SWA_GROUP * grp) // heads_per_kv
        vt_band = vt[kv * SWA_HEAD_DIM:(kv + 1) * SWA_HEAD_DIM, band]
        ot = [_dot(vt_band, pairs[pair]) for pair in range(2)]
        ot = jnp.concatenate([ot[0][:, :BLOCK], ot[0][:, BLOCK:],
                              ot[1][:, :BLOCK], ot[1][:, BLOCK:]], axis=0)
        o = ot.T
        gate = gate_slab[grp][blk * BLOCK:(blk + 1) * BLOCK]
        y_blocks[grp][blk] = (o * (gate * jax.nn.sigmoid(gate))).astype(BF16)

    def out_project(grp):
        y = jnp.concatenate(y_blocks[grp], axis=0)
        part = _dot_cols(y, w_out[grp * out_per_group:(grp + 1) * out_per_group])
        if grp == 0:
            o_ref[0] = x + part
        else:
            o_ref[0] += part

    items = [(blk, grp) for grp in range(n_groups) for blk in range(n_blocks)]
    last = len(items) - 1
    jobs = []
    for grp in range(n_groups):
        jobs.append((functools.partial(project_gate, grp), 0, grp * n_blocks + 1))
        if grp + 1 < n_groups:
            jobs.append((functools.partial(project_q, grp + 1), 0, (grp + 1) * n_blocks - 2))
        if grp >= 1:
            jobs.append((functools.partial(out_project, grp - 1), grp * n_blocks + 1, last))
    jobs += [(prefetch_h, 0, last), (prefetch_q0, n_blocks - 1, last)]
    pairs = [jobs[n:n + 2] for n in range(0, len(jobs), 2)]
    fillers = {}
    slot = -1
    for n, pair in enumerate(pairs):
        slot = max(slot + 1, (n * len(items)) // len(pairs), *(first for _, first, _ in pair))
        assert all(slot <= last_item for _, _, last_item in pair), "filler after its consumer"
        fillers[slot] = [job for job, _, _ in pair]
    q_slab[0] = q0_ref
    pending = None
    for job in fillers.pop(0, []):
        job()
    tiles = scores(*items[0])
    for idx, (blk, grp) in enumerate(items):
        next_tiles = scores(*items[idx + 1]) if idx + 1 < len(items) else None
        for job in fillers.get(idx, []):
            job()
        if pending is not None:
            finish(*pending)
        pending = (blk, grp, softmax(blk, grp, tiles))
        tiles = next_tiles
    finish(*pending)
    out_project(n_groups - 1)


def _swa_layer(x, g, w_in, g_q, sinks, w_out, layer, k, vt, bias):
    b, s, d = x.shape
    t = SWA_TILE
    in_blocks = [_weight_block(layer, d, 0, c) for c in range(SWA_IN_BLOCKS)]
    out_blocks = [_weight_block(layer, SLAB, grp, c)
                  for grp in range(SWA_GROUPS) for c in range(d // SLAB)]

    def prev_block(j):
        return jnp.maximum(j * (t // BLOCK) - 1, 0)

    k_cur = pl.BlockSpec((1, t, KV_WIDTH), lambda i, j: (i, j, 0))
    k_prev = pl.BlockSpec((1, BLOCK, KV_WIDTH), lambda i, j: (i, prev_block(j), 0))
    vt_cur = pl.BlockSpec((1, KV_WIDTH, t), lambda i, j: (i, 0, j))
    vt_prev = pl.BlockSpec((1, KV_WIDTH, BLOCK), lambda i, j: (i, 0, prev_block(j)))
    n_tiles = s // t
    return pl.pallas_call(
        _swa_kernel,
        grid=(b, n_tiles),
        in_specs=[
            pl.BlockSpec(memory_space=pltpu.SMEM),
            pl.BlockSpec((1, t, d), lambda i, j: (i, j, 0)),
            _next_tile_spec(b, n_tiles, t, d),
            _resident((1, d)),
            _resident((LANES, BLOCK)),
            *in_blocks, *out_blocks,
            k_cur, k_prev, vt_cur, vt_prev,
            _resident(bias.shape),
        ],
        out_specs=pl.BlockSpec((1, t, d), lambda i, j: (i, j, 0)),
        out_shape=jax.ShapeDtypeStruct(x.shape, x.dtype),
        scratch_shapes=[pltpu.VMEM((t, d), BF16), pltpu.VMEM((t, SLAB), F32)],
        compiler_params=pltpu.CompilerParams(
            dimension_semantics=("arbitrary", "arbitrary"),
            vmem_limit_bytes=VMEM_LIMIT_BYTES),
        name="swa_layer",
    )(sinks, x, x.reshape(b * n_tiles, t, d), g.reshape(1, d),
      jnp.broadcast_to((jnp.tile(g_q, 2) * (SWA_HEAD_DIM ** -0.5))[:, None], (LANES, BLOCK)),
      *([w_in] * SWA_IN_BLOCKS), *([w_out] * SWA_OUT_BLOCKS), k, k, vt, vt, bias)


def kernel(x, a_norm_g, a_w_in, a_w_out, kv_norm_g, w_kv, k_norm_g, rel_bias,
           b_norm_g, b_w_in, b_q_norm_g, b_sinks, b_w_out):
    s = x.shape[1]
    tables = _retention_tables(s)
    a_w_in, a_w_out = a_w_in.astype(BF16), a_w_out.astype(BF16)
    b_w_in, b_w_out = b_w_in.astype(BF16), b_w_out.astype(BF16)
    for layer in range(a_w_in.shape[0]):
        x = _retention_layer(x, a_norm_g[layer], a_w_in, a_w_out, layer, *tables)
    k, vt = _shared_kv(x, kv_norm_g, w_kv, k_norm_g)
    bias = _bias_tables(rel_bias)
    for layer in range(b_w_in.shape[0]):
        x = _swa_layer(x, b_norm_g[layer], b_w_in, b_q_norm_g[layer], b_sinks[layer], b_w_out,
                       layer, k, vt, bias)
    return x
```

```python
import functools
import math

import jax
import jax.numpy as jnp
from jax import lax
from jax.experimental import pallas as pl
from jax.experimental.pallas import tpu as pltpu

D_MODEL = 1024
RET_HEADS = 4
RET_QK_DIM = D_MODEL // RET_HEADS
RET_V_DIM = 2 * RET_QK_DIM
RET_V_TOTAL = RET_HEADS * RET_V_DIM
RET_CHUNK = 128
A_IN_WIDTH = 2 * D_MODEL + 2 * RET_V_TOTAL
SWA_HEAD_DIM = 64
SWA_Q_HEADS = D_MODEL // SWA_HEAD_DIM
SWA_KV_HEADS = SWA_Q_HEADS // 8
SWA_Q_WIDTH = SWA_Q_HEADS * SWA_HEAD_DIM
KV_WIDTH = SWA_KV_HEADS * SWA_HEAD_DIM
WINDOW = 128
BLOCK = 128
REL_BUCKETS = 32
REL_MAX_DIST = 128
EPS = 1e-6

LANES = 128
SLAB = 256
VMEM_LIMIT_BYTES = 56 * 1024 * 1024

RET_TILE = 512
KV_TILE = 2048
SWA_TILE = 1024
SWA_GROUP = 4

BF16 = jnp.bfloat16
F32 = jnp.float32


def _dot(a, b):
    return jnp.dot(a, b, preferred_element_type=F32)


def _dot_nt(a, b):
    return lax.dot_general(a, b, (((1,), (1,)), ((), ())), preferred_element_type=F32)


def _dot_tn(a, b):
    return lax.dot_general(a, b, (((0,), (0,)), ((), ())), preferred_element_type=F32)


def _dot_cols(a, w_refs):
    return jnp.concatenate([_dot(a, w[...]) for w in w_refs], axis=1)


def _silu(x):
    return x * (0.5 * jnp.tanh(0.5 * x) + 0.5)


def _rms(x):
    return x * lax.rsqrt(jnp.mean(x * x, axis=-1, keepdims=True) + EPS)


def _resident(shape):
    nd = len(shape)
    return pl.BlockSpec(shape, lambda *_: (0,) * nd, pipeline_mode=pl.Buffered(1))


def _next_tile_spec(b, n_tiles, t, d):
    last = b * n_tiles - 1
    return pl.BlockSpec((1, t, d), lambda i, j: (jnp.minimum(i * n_tiles + j + 1, last), 0, 0))


def _weight_block(layer, rows, row_block, col_block):
    return pl.BlockSpec((None, rows, SLAB), lambda *_: (layer, row_block, col_block),
                        pipeline_mode=pl.Buffered(1))


RET_IN_BLOCKS = A_IN_WIDTH // SLAB
RET_OUT_BLOCKS = RET_HEADS * (D_MODEL // SLAB)


def _ret_kernel(x_ref, xnext_ref, g_ref, *refs):
    w_in = refs[:RET_IN_BLOCKS]
    w_out = refs[RET_IN_BLOCKS:RET_IN_BLOCKS + RET_OUT_BLOCKS]
    (perm_ref, cos_ref, sin_ref, idec_ref, qdec_ref, kdec_ref, cdec_ref,
     o_ref, state_ref, h_ref, wqk_ref) = refs[RET_IN_BLOCKS + RET_OUT_BLOCKS:]
    qk_blocks = D_MODEL // SLAB
    v_per_head = RET_V_DIM // SLAB
    out_per_head = D_MODEL // SLAB
    assert RET_QK_DIM == SLAB

    def wq(hd):
        return [wqk_ref.at[hd]]

    def wk(hd):
        return [wqk_ref.at[qk_blocks + hd]]

    def wv(hd):
        start = 2 * qk_blocks + hd * v_per_head
        return w_in[start:start + v_per_head]

    def wgate(hd):
        start = 2 * qk_blocks + RET_HEADS * v_per_head + hd * v_per_head
        return w_in[start:start + v_per_head]

    @pl.when(pl.program_id(1) == 0)
    def _():
        state_ref[...] = jnp.zeros_like(state_ref)

    def normed(ref):
        return (_rms(ref[0]) * g_ref[...]).astype(BF16)

    @pl.when((pl.program_id(0) == 0) & (pl.program_id(1) == 0))
    def _():
        h_ref[...] = normed(x_ref)
        for blk in range(2 * qk_blocks):
            wqk_ref[blk] = _dot(w_in[blk][...], perm_ref[...]).astype(BF16)

    def prefetch_h():
        h_ref[...] = normed(xnext_ref)

    x = x_ref[0]
    cos = cos_ref[...]
    sin = sin_ref[...]
    half = RET_QK_DIM // 2
    n_chunks = x.shape[0] // RET_CHUNK

    def rotate(t):
        t1, t2 = t[:, :half], t[:, half:]
        return jnp.concatenate([t1 * cos - t2 * sin, t2 * cos + t1 * sin], axis=1)

    proj = [dict() for _ in range(RET_HEADS)]

    def project_q(hd):
        proj[hd]["q"] = rotate(_dot_cols(h_ref[...], wq(hd))).astype(BF16)

    def project_k(hd):
        k = rotate(_dot_cols(h_ref[...], wk(hd))) * (RET_QK_DIM ** -0.5)
        proj[hd]["k"] = k.astype(BF16)
        kdec = jnp.concatenate([kdec_ref[hd]] * n_chunks, axis=0)
        proj[hd]["kd"] = (k * kdec).astype(BF16)

    def project_v(hd):
        proj[hd]["v"] = _dot_cols(h_ref[...], wv(hd)).astype(BF16)

    def project_gate(hd):
        proj[hd]["gate"] = _dot_cols(h_ref[...], wgate(hd))

    def finish(hd, outs, acc):
        o = _rms(jnp.concatenate(outs, axis=0))
        gate = proj[hd]["gate"]
        y = (o * _silu(gate)).astype(BF16)
        return acc + _dot_cols(y, w_out[hd * out_per_head:(hd + 1) * out_per_head])

    project_q(0)
    project_k(0)
    project_v(0)
    acc = x
    pending = None
    for hd in range(RET_HEADS):
        fillers = [functools.partial(project_gate, hd)]
        if hd + 1 < RET_HEADS:
            fillers += [functools.partial(f, hd + 1) for f in (project_q, project_k, project_v)]
        else:
            fillers += [prefetch_h]
        qb, kb, kd, v = proj[hd]["q"], proj[hd]["k"], proj[hd]["kd"], proj[hd]["v"]
        st = state_ref[hd]
        outs = []
        for c in range(n_chunks):
            rows = slice(c * RET_CHUNK, (c + 1) * RET_CHUNK)
            qc, kc, vc = qb[rows], kb[rows], v[rows]
            scores = _dot_nt(qc, kc) * idec_ref[hd]
            inter = _dot(qc, st.astype(BF16)) * qdec_ref[hd]
            st = st * cdec_ref[hd] + _dot_tn(kd[rows], vc)
            lo_f = (c * len(fillers)) // n_chunks
            hi_f = ((c + 1) * len(fillers)) // n_chunks
            for f in fillers[lo_f:hi_f]:
                f()
            if c == 0 and pending is not None:
                acc = finish(*pending, acc)
            outs.append(_dot(scores.astype(BF16), vc) + inter)
        state_ref[hd] = st
        pending = (hd, outs)
    o_ref[0] = finish(*pending, acc)


def _retention_layer(x, g, w_in, w_out, layer, cos, sin, idec, qdec, kdec, cdec):
    b, s, d = x.shape
    t = RET_TILE
    in_blocks = [_weight_block(layer, d, 0, c) for c in range(RET_IN_BLOCKS)]
    out_blocks = [_weight_block(layer, RET_V_DIM, hd, c)
                  for hd in range(RET_HEADS) for c in range(d // SLAB)]
    n_tiles = s // t
    half = RET_QK_DIM // 2
    src = jnp.arange(RET_QK_DIM)
    dst = jnp.where(src % 2 == 0, src // 2, half + src // 2)
    perm = (dst[:, None] == jnp.arange(RET_QK_DIM)[None, :]).astype(BF16)
    return pl.pallas_call(
        _ret_kernel,
        grid=(b, n_tiles),
        in_specs=[
            pl.BlockSpec((1, t, d), lambda i, j: (i, j, 0)),
            _next_tile_spec(b, n_tiles, t, d),
            _resident((1, d)),
            *in_blocks, *out_blocks,
            _resident(perm.shape),
            pl.BlockSpec((t, half), lambda i, j: (j, 0)),
            pl.BlockSpec((t, half), lambda i, j: (j, 0)),
            _resident(idec.shape), _resident(qdec.shape), _resident(kdec.shape),
            _resident(cdec.shape),
        ],
        out_specs=pl.BlockSpec((1, t, d), lambda i, j: (i, j, 0)),
        out_shape=jax.ShapeDtypeStruct(x.shape, x.dtype),
        scratch_shapes=[pltpu.VMEM((RET_HEADS, RET_QK_DIM, RET_V_DIM), F32),
                        pltpu.VMEM((t, d), BF16),
                        pltpu.VMEM((2 * d // SLAB, d, SLAB), BF16)],
        compiler_params=pltpu.CompilerParams(
            dimension_semantics=("arbitrary", "arbitrary"),
            vmem_limit_bytes=VMEM_LIMIT_BYTES),
        name="retention_layer",
    )(x, x.reshape(b * n_tiles, t, d), g.reshape(1, d),
      *([w_in] * RET_IN_BLOCKS), *([w_out] * RET_OUT_BLOCKS),
      perm, cos, sin, idec, qdec, kdec, cdec)


def _retention_tables(s):
    dk = RET_QK_DIM
    angle = 1.0 / (10000.0 ** jnp.linspace(0.0, 1.0, dk // 2, dtype=F32))
    pos = jnp.arange(s, dtype=F32)[:, None]
    sin, cos = jnp.sin(pos * angle), jnp.cos(pos * angle)
    c = RET_CHUNK
    log_gamma = jnp.log(1.0 - 2.0 ** (-5.0 - jnp.arange(RET_HEADS, dtype=F32)))
    idx = jnp.arange(c, dtype=F32)
    diff = idx[:, None] - idx[None, :]
    idec = jnp.where(diff[None] >= 0,
                     jnp.exp(jnp.maximum(diff, 0.0)[None] * log_gamma[:, None, None]), 0.0)
    qdec = jnp.exp((idx + 1.0)[None, :, None] * log_gamma[:, None, None])
    kdec = jnp.exp((c - 1.0 - idx)[None, :, None] * log_gamma[:, None, None])
    cdec = jnp.exp(c * log_gamma)[:, None, None]
    return cos, sin, idec.astype(F32), qdec, kdec, cdec


def _kv_kernel(x_ref, g_ref, w_ref, gk_ref, k_ref, vt_ref):
    h = (_rms(x_ref[0]) * g_ref[...]).astype(BF16)
    kv = _dot(h, w_ref[...])
    k, v = kv[:, :KV_WIDTH], kv[:, KV_WIDTH:]
    lo = lax.broadcasted_iota(jnp.int32, (1, KV_WIDTH), 1) < SWA_HEAD_DIM
    sq = k * k
    ms_lo = jnp.sum(jnp.where(lo, sq, 0.0), axis=-1, keepdims=True) / SWA_HEAD_DIM
    ms_hi = jnp.sum(jnp.where(lo, 0.0, sq), axis=-1, keepdims=True) / SWA_HEAD_DIM
    r = jnp.where(lo, lax.rsqrt(ms_lo + EPS), lax.rsqrt(ms_hi + EPS))
    k_ref[0] = k * r * gk_ref[...]
    vt_ref[0] = v.T


def _shared_kv(x, g, w_kv, g_k):
    b, s, d = x.shape
    t = KV_TILE
    return pl.pallas_call(
        _kv_kernel,
        grid=(b, s // t),
        in_specs=[
            pl.BlockSpec((1, t, d), lambda i, j: (i, j, 0)),
            _resident((1, d)),
            _resident(w_kv.shape),
            _resident((1, KV_WIDTH)),
        ],
        out_specs=[pl.BlockSpec((1, t, KV_WIDTH), lambda i, j: (i, j, 0)),
                   pl.BlockSpec((1, KV_WIDTH, t), lambda i, j: (i, 0, j))],
        out_shape=[jax.ShapeDtypeStruct((b, s, KV_WIDTH), F32),
                   jax.ShapeDtypeStruct((b, KV_WIDTH, s), F32)],
        compiler_params=pltpu.CompilerParams(
            dimension_semantics=("parallel", "parallel"),
            vmem_limit_bytes=VMEM_LIMIT_BYTES),
        name="shared_kv",
    )(x, g.reshape(1, d), w_kv.astype(BF16), jnp.tile(g_k, SWA_KV_HEADS).reshape(1, KV_WIDTH))


def _prev_key_valid():
    j = lax.broadcasted_iota(jnp.int32, (BLOCK, BLOCK), 0)
    i = lax.broadcasted_iota(jnp.int32, (BLOCK, BLOCK), 1)
    return j > i, j, i


def _bias_kernel(rel_ref, o_ref):
    hd = pl.program_id(0)
    prev, j, i = _prev_key_valid()
    dist = jnp.where(prev, i + BLOCK - j, i - j)
    max_exact = REL_BUCKETS // 2
    dist_f = jnp.maximum(dist, 1).astype(F32)
    large = max_exact + jnp.floor(jnp.log(dist_f / max_exact) / math.log(REL_MAX_DIST / max_exact)
                                  * (REL_BUCKETS - max_exact)).astype(jnp.int32)
    large = jnp.minimum(large, REL_BUCKETS - 1)
    bucket = jnp.where(dist < max_exact, dist, large)
    bias = jnp.zeros((BLOCK, BLOCK), F32)
    for bk in range(REL_BUCKETS):
        bias = jnp.where(bucket == bk, rel_ref[bk, hd], bias)
    o_ref[0, 0] = bias
    o_ref[1, 0] = jnp.where(prev, jnp.float32(-jnp.inf), bias)


def _bias_tables(rel_bias):
    return pl.pallas_call(
        _bias_kernel,
        grid=(SWA_Q_HEADS,),
        in_specs=[pl.BlockSpec(memory_space=pltpu.SMEM)],
        out_specs=pl.BlockSpec((2, 1, BLOCK, BLOCK), lambda h: (0, h, 0, 0)),
        out_shape=jax.ShapeDtypeStruct((2, SWA_Q_HEADS, BLOCK, BLOCK), F32),
        name="rel_bias_tables",
    )(rel_bias)


SWA_GROUPS = SWA_Q_HEADS // SWA_GROUP
SWA_IN_BLOCKS = 2 * SWA_Q_WIDTH // SLAB
SWA_OUT_BLOCKS = SWA_GROUPS * (D_MODEL // SLAB)
assert SWA_GROUP * SWA_HEAD_DIM == SLAB


def _swa_kernel(sinks_ref, x_ref, xnext_ref, g_ref, gq_ref, *refs):
    w_in = refs[:SWA_IN_BLOCKS]
    w_out = refs[SWA_IN_BLOCKS:SWA_IN_BLOCKS + SWA_OUT_BLOCKS]
    (kc_ref, kp_ref, vtc_ref, vtp_ref, bias_ref,
     o_ref, h_ref, q0_ref) = refs[SWA_IN_BLOCKS + SWA_OUT_BLOCKS:]
    out_per_group = D_MODEL // SLAB

    def normed(ref):
        return (_rms(ref[0]) * g_ref[...]).astype(BF16)

    @pl.when((pl.program_id(0) == 0) & (pl.program_id(1) == 0))
    def _():
        h_ref[...] = normed(x_ref)
        q0_ref[...] = _dot(h_ref[...], w_in[0][...])

    def prefetch_h():
        h_ref[...] = normed(xnext_ref)

    def prefetch_q0():
        q0_ref[...] = _dot(h_ref[...], w_in[0][...])

    first = (pl.program_id(1) == 0).astype(jnp.int32)
    x = x_ref[0]
    n_blocks = x.shape[0] // BLOCK
    n_groups = SWA_GROUPS

    lo = lax.broadcasted_iota(jnp.int32, (1, LANES), 1) < SWA_HEAD_DIM
    lo_rows = lax.broadcasted_iota(jnp.int32, (LANES, 1), 0) < SWA_HEAD_DIM

    kall = jnp.concatenate([kp_ref[0], kc_ref[0]], axis=0)
    kswap = pltpu.roll(kall, SWA_HEAD_DIM, 1)
    kdup = [jnp.where(lo, kall, kswap).astype(BF16), jnp.where(lo, kswap, kall).astype(BF16)]
    vt = jnp.concatenate([vtp_ref[0], vtc_ref[0]], axis=1).astype(BF16)
    gq = gq_ref[...]
    prev_valid, _, _ = _prev_key_valid()

    heads_per_kv = SWA_Q_HEADS // SWA_KV_HEADS

    q_slab = [None] * n_groups
    gate_slab = [None] * n_groups
    y_blocks = [[None] * n_blocks for _ in range(n_groups)]

    def project_q(grp):
        q_slab[grp] = _dot(h_ref[...], w_in[grp][...])

    def project_gate(grp):
        gate_slab[grp] = _dot(h_ref[...], w_in[n_groups + grp][...])

    def scores(blk, grp):
        band = slice(blk * BLOCK, (blk + 2) * BLOCK)
        kv = (SWA_GROUP * grp) // heads_per_kv
        tiles = []
        for pair in range(2):
            qp = q_slab[grp][blk * BLOCK:(blk + 1) * BLOCK, pair * LANES:(pair + 1) * LANES]
            qt = qp.T
            sq = qt * qt
            ms_lo = jnp.sum(sq[:SWA_HEAD_DIM], axis=0, keepdims=True) / SWA_HEAD_DIM
            ms_hi = jnp.sum(sq[SWA_HEAD_DIM:], axis=0, keepdims=True) / SWA_HEAD_DIM
            r = jnp.where(lo_rows, lax.rsqrt(ms_lo + EPS), lax.rsqrt(ms_hi + EPS))
            qt = qt * r * gq
            qstack = jnp.concatenate([jnp.where(lo_rows, qt, 0.0), jnp.where(lo_rows, 0.0, qt)],
                                     axis=1).astype(BF16)
            tiles.append(_dot(kdup[kv][band], qstack))
        return tiles

    def softmax(blk, grp, tiles):
        table = first if blk == 0 else 0
        pairs = []
        for pair in range(2):
            heads = []
            for pos in range(2):
                hd = SWA_GROUP * grp + 2 * pair + pos
                sink = sinks_ref[hd]
                cols = slice(pos * BLOCK, (pos + 1) * BLOCK)
                st = tiles[pair]
                s = (jnp.where(prev_valid, st[:BLOCK, cols], st[BLOCK:, cols])
                     + bias_ref[table, hd])
                m = jnp.maximum(jnp.max(s, axis=0, keepdims=True), sink)
                e = jnp.exp(s - m)
                den = jnp.sum(e, axis=0, keepdims=True) + jnp.exp(sink - m)
                pn = e * (1.0 / den)
                heads.append(jnp.concatenate([jnp.where(prev_valid, pn, 0.0),
                                              jnp.where(prev_valid, 0.0, pn)],
                                             axis=0).astype(BF16))
            pairs.append(jnp.concatenate(heads, axis=1))
        return pairs

    def finish(blk, grp, pairs):
        band = slice(blk * BLOCK, (blk + 2) * BLOCK)
        kv = (SWA_GROUP * grp) // heads_per_kv
        vt_band = vt[kv * SWA_HEAD_DIM:(kv + 1) * SWA_HEAD_DIM, band]
        ot = [_dot(vt_band, pairs[pair]) for pair in range(2)]
        ot = jnp.concatenate([ot[0][:, :BLOCK], ot[0][:, BLOCK:],
                              ot[1][:, :BLOCK], ot[1][:, BLOCK:]], axis=0)
        o = ot.T
        gate = gate_slab[grp][blk * BLOCK:(blk + 1) * BLOCK]
        y_blocks[grp][blk] = (o * _silu(gate)).astype(BF16)

    def out_project(grp):
        y = jnp.concatenate(y_blocks[grp], axis=0)
        part = _dot_cols(y, w_out[grp * out_per_group:(grp + 1) * out_per_group])
        if grp == 0:
            o_ref[0] = x + part
        else:
            o_ref[0] += part

    items = [(blk, grp) for grp in range(n_groups) for blk in range(n_blocks)]
    last = len(items) - 1
    jobs = []
    for grp in range(n_groups):
        jobs.append((functools.partial(project_gate, grp), 0, grp * n_blocks + 1))
        if grp + 1 < n_groups:
            jobs.append((functools.partial(project_q, grp + 1), 0, (grp + 1) * n_blocks - 2))
        if grp >= 1:
            jobs.append((functools.partial(out_project, grp - 1), grp * n_blocks + 1, last))
    jobs += [(prefetch_h, 0, last), (prefetch_q0, n_blocks - 1, last)]
    pairs = [jobs[n:n + 2] for n in range(0, len(jobs), 2)]
    fillers = {}
    slot = -1
    for n, pair in enumerate(pairs):
        slot = max(slot + 1, (n * len(items)) // len(pairs), *(first for _, first, _ in pair))
        assert all(slot <= last_item for _, _, last_item in pair), "filler after its consumer"
        fillers[slot] = [job for job, _, _ in pair]
    q_slab[0] = q0_ref
    pending = None
    for job in fillers.pop(0, []):
        job()
    tiles = scores(*items[0])
    for idx, (blk, grp) in enumerate(items):
        next_tiles = scores(*items[idx + 1]) if idx + 1 < len(items) else None
        for job in fillers.get(idx, []):
            job()
        if pending is not None:
            finish(*pending)
        pending = (blk, grp, softmax(blk, grp, tiles))
        tiles = next_tiles
    finish(*pending)
    out_project(n_groups - 1)


def _swa_layer(x, g, w_in, g_q, sinks, w_out, layer, k, vt, bias):
    b, s, d = x.shape
    t = SWA_TILE
    in_blocks = [_weight_block(layer, d, 0, c) for c in range(SWA_IN_BLOCKS)]
    out_blocks = [_weight_block(layer, SLAB, grp, c)
                  for grp in range(SWA_GROUPS) for c in range(d // SLAB)]

    def prev_block(j):
        return jnp.maximum(j * (t // BLOCK) - 1, 0)

    k_cur = pl.BlockSpec((1, t, KV_WIDTH), lambda i, j: (i, j, 0))
    k_prev = pl.BlockSpec((1, BLOCK, KV_WIDTH), lambda i, j: (i, prev_block(j), 0))
    vt_cur = pl.BlockSpec((1, KV_WIDTH, t), lambda i, j: (i, 0, j))
    vt_prev = pl.BlockSpec((1, KV_WIDTH, BLOCK), lambda i, j: (i, 0, prev_block(j)))
    n_tiles = s // t
    return pl.pallas_call(
        _swa_kernel,
        grid=(b, n_tiles),
        in_specs=[
            pl.BlockSpec(memory_space=pltpu.SMEM),
            pl.BlockSpec((1, t, d), lambda i, j: (i, j, 0)),
            _next_tile_spec(b, n_tiles, t, d),
            _resident((1, d)),
            _resident((LANES, BLOCK)),
            *in_blocks, *out_blocks,
            k_cur, k_prev, vt_cur, vt_prev,
            _resident(bias.shape),
        ],
        out_specs=pl.BlockSpec((1, t, d), lambda i, j: (i, j, 0)),
        out_shape=jax.ShapeDtypeStruct(x.shape, x.dtype),
        scratch_shapes=[pltpu.VMEM((t, d), BF16), pltpu.VMEM((t, SLAB), F32)],
        compiler_params=pltpu.CompilerParams(
            dimension_semantics=("arbitrary", "arbitrary"),
            vmem_limit_bytes=VMEM_LIMIT_BYTES),
        name="swa_layer",
    )(sinks, x, x.reshape(b * n_tiles, t, d), g.reshape(1, d),
      jnp.broadcast_to((jnp.tile(g_q, 2) * (SWA_HEAD_DIM ** -0.5))[:, None], (LANES, BLOCK)),
      *([w_in] * SWA_IN_BLOCKS), *([w_out] * SWA_OUT_BLOCKS), k, k, vt, vt, bias)


def kernel(x, a_norm_g, a_w_in, a_w_out, kv_norm_g, w_kv, k_norm_g, rel_bias,
           b_norm_g, b_w_in, b_q_norm_g, b_sinks, b_w_out):
    s = x.shape[1]
    tables = _retention_tables(s)
    a_w_in, a_w_out = a_w_in.astype(BF16), a_w_out.astype(BF16)
    b_w_in, b_w_out = b_w_in.astype(BF16), b_w_out.astype(BF16)
    for layer in range(a_w_in.shape[0]):
        x = _retention_layer(x, a_norm_g[layer], a_w_in, a_w_out, layer, *tables)
    k, vt = _shared_kv(x, kv_norm_g, w_kv, k_norm_g)
    bias = _bias_tables(rel_bias)
    for layer in range(b_w_in.shape[0]):
        x = _swa_layer(x, b_norm_g[layer], b_w_in, b_q_norm_g[layer], b_sinks[layer], b_w_out,
                       layer, k, vt, bias)
    return x
```

```python
import functools
import math

import jax
import jax.numpy as jnp
from jax import lax
from jax.experimental import pallas as pl
from jax.experimental.pallas import tpu as pltpu

D_MODEL = 1024
RET_HEADS = 4
RET_QK_DIM = D_MODEL // RET_HEADS
RET_V_DIM = 2 * RET_QK_DIM
RET_V_TOTAL = RET_HEADS * RET_V_DIM
RET_CHUNK = 128
A_IN_WIDTH = 2 * D_MODEL + 2 * RET_V_TOTAL
SWA_HEAD_DIM = 64
SWA_Q_HEADS = D_MODEL // SWA_HEAD_DIM
SWA_KV_HEADS = SWA_Q_HEADS // 8
SWA_Q_WIDTH = SWA_Q_HEADS * SWA_HEAD_DIM
KV_WIDTH = SWA_KV_HEADS * SWA_HEAD_DIM
WINDOW = 128
BLOCK = 128
REL_BUCKETS = 32
REL_MAX_DIST = 128
EPS = 1e-6

LANES = 128
SLAB = 256
VMEM_LIMIT_BYTES = 56 * 1024 * 1024

RET_TILE = 512
KV_TILE = 2048
SWA_TILE = 1024
SWA_GROUP = 4

BF16 = jnp.bfloat16
F32 = jnp.float32


def _dot(a, b):
    return jnp.dot(a, b, preferred_element_type=F32)


def _dot_nt(a, b):
    return lax.dot_general(a, b, (((1,), (1,)), ((), ())), preferred_element_type=F32)


def _dot_tn(a, b):
    return lax.dot_general(a, b, (((0,), (0,)), ((), ())), preferred_element_type=F32)


def _dot_cols(a, w_refs):
    return jnp.concatenate([_dot(a, w[...]) for w in w_refs], axis=1)


def _gated(o, x):
    h = 0.5 * x
    return (o * h) * (jnp.tanh(h) + 1.0)


def _rms(x):
    return x * lax.rsqrt(jnp.mean(x * x, axis=-1, keepdims=True) + EPS)


def _resident(shape):
    nd = len(shape)
    return pl.BlockSpec(shape, lambda *_: (0,) * nd, pipeline_mode=pl.Buffered(1))


def _next_tile_spec(b, n_tiles, t, d):
    last = b * n_tiles - 1
    return pl.BlockSpec((1, t, d), lambda i, j: (jnp.minimum(i * n_tiles + j + 1, last), 0, 0))


def _weight_block(layer, rows, row_block, col_block):
    return pl.BlockSpec((None, rows, SLAB), lambda *_: (layer, row_block, col_block),
                        pipeline_mode=pl.Buffered(1))


RET_IN_BLOCKS = A_IN_WIDTH // SLAB
RET_OUT_BLOCKS = RET_HEADS * (D_MODEL // SLAB)


def _ret_kernel(x_ref, xnext_ref, g_ref, *refs):
    w_in = refs[:RET_IN_BLOCKS]
    w_out = refs[RET_IN_BLOCKS:RET_IN_BLOCKS + RET_OUT_BLOCKS]
    (perm_ref, cos_ref, sin_ref, idec_ref, qdec_ref, kdec_ref, cdec_ref,
     o_ref, state_ref, h_ref, wqk_ref) = refs[RET_IN_BLOCKS + RET_OUT_BLOCKS:]
    qk_blocks = D_MODEL // SLAB
    v_per_head = RET_V_DIM // SLAB
    out_per_head = D_MODEL // SLAB
    assert RET_QK_DIM == SLAB

    def wq(hd):
        return [wqk_ref.at[hd]]

    def wk(hd):
        return [wqk_ref.at[qk_blocks + hd]]

    def wv(hd):
        start = 2 * qk_blocks + hd * v_per_head
        return w_in[start:start + v_per_head]

    def wgate(hd):
        start = 2 * qk_blocks + RET_HEADS * v_per_head + hd * v_per_head
        return w_in[start:start + v_per_head]

    @pl.when(pl.program_id(1) == 0)
    def _():
        state_ref[...] = jnp.zeros_like(state_ref)

    def normed(ref):
        return (_rms(ref[0]) * g_ref[...]).astype(BF16)

    @pl.when((pl.program_id(0) == 0) & (pl.program_id(1) == 0))
    def _():
        h_ref[...] = normed(x_ref)
        for blk in range(2 * qk_blocks):
            wqk_ref[blk] = _dot(w_in[blk][...], perm_ref[...]).astype(BF16)

    def prefetch_h():
        h_ref[...] = normed(xnext_ref)

    x = x_ref[0]
    cos = cos_ref[...]
    sin = sin_ref[...]
    half = RET_QK_DIM // 2
    n_chunks = x.shape[0] // RET_CHUNK

    def rotate(t):
        t1, t2 = t[:, :half], t[:, half:]
        return jnp.concatenate([t1 * cos - t2 * sin, t2 * cos + t1 * sin], axis=1)

    proj = [dict() for _ in range(RET_HEADS)]

    def project_q(hd):
        proj[hd]["q"] = rotate(_dot_cols(h_ref[...], wq(hd))).astype(BF16)

    def project_k(hd):
        k = rotate(_dot_cols(h_ref[...], wk(hd))) * (RET_QK_DIM ** -0.5)
        proj[hd]["k"] = k.astype(BF16)
        kdec = jnp.concatenate([kdec_ref[hd]] * n_chunks, axis=0)
        proj[hd]["kd"] = (k * kdec).astype(BF16)

    def project_v(hd):
        proj[hd]["v"] = _dot_cols(h_ref[...], wv(hd)).astype(BF16)

    def project_gate(hd):
        proj[hd]["gate"] = _dot_cols(h_ref[...], wgate(hd))

    def finish(hd, outs, acc):
        o = _rms(jnp.concatenate(outs, axis=0))
        gate = proj[hd]["gate"]
        y = _gated(o, gate).astype(BF16)
        return acc + _dot_cols(y, w_out[hd * out_per_head:(hd + 1) * out_per_head])

    project_q(0)
    project_k(0)
    project_v(0)
    acc = x
    pending = None
    for hd in range(RET_HEADS):
        fillers = [functools.partial(project_gate, hd)]
        if hd + 1 < RET_HEADS:
            fillers += [functools.partial(f, hd + 1) for f in (project_q, project_k, project_v)]
        else:
            fillers += [prefetch_h]
        qb, kb, kd, v = proj[hd]["q"], proj[hd]["k"], proj[hd]["kd"], proj[hd]["v"]
        st = state_ref[hd]
        outs = []
        for c in range(n_chunks):
            rows = slice(c * RET_CHUNK, (c + 1) * RET_CHUNK)
            qc, kc, vc = qb[rows], kb[rows], v[rows]
            scores = _dot_nt(qc, kc) * idec_ref[hd]
            inter = _dot(qc, st.astype(BF16)) * qdec_ref[hd]
            st = st * cdec_ref[hd] + _dot_tn(kd[rows], vc)
            lo_f = (c * len(fillers)) // n_chunks
            hi_f = ((c + 1) * len(fillers)) // n_chunks
            for f in fillers[lo_f:hi_f]:
                f()
            if c == 0 and pending is not None:
                acc = finish(*pending, acc)
            outs.append(_dot(scores.astype(BF16), vc) + inter)
        state_ref[hd] = st
        pending = (hd, outs)
    o_ref[0] = finish(*pending, acc)


def _retention_layer(x, g, w_in, w_out, layer, cos, sin, idec, qdec, kdec, cdec):
    b, s, d = x.shape
    t = RET_TILE
    in_blocks = [_weight_block(layer, d, 0, c) for c in range(RET_IN_BLOCKS)]
    out_blocks = [_weight_block(layer, RET_V_DIM, hd, c)
                  for hd in range(RET_HEADS) for c in range(d // SLAB)]
    n_tiles = s // t
    half = RET_QK_DIM // 2
    src = jnp.arange(RET_QK_DIM)
    dst = jnp.where(src % 2 == 0, src // 2, half + src // 2)
    perm = (dst[:, None] == jnp.arange(RET_QK_DIM)[None, :]).astype(BF16)
    return pl.pallas_call(
        _ret_kernel,
        grid=(b, n_tiles),
        in_specs=[
            pl.BlockSpec((1, t, d), lambda i, j: (i, j, 0)),
            _next_tile_spec(b, n_tiles, t, d),
            _resident((1, d)),
            *in_blocks, *out_blocks,
            _resident(perm.shape),
            pl.BlockSpec((t, half), lambda i, j: (j, 0)),
            pl.BlockSpec((t, half), lambda i, j: (j, 0)),
            _resident(idec.shape), _resident(qdec.shape), _resident(kdec.shape),
            _resident(cdec.shape),
        ],
        out_specs=pl.BlockSpec((1, t, d), lambda i, j: (i, j, 0)),
        out_shape=jax.ShapeDtypeStruct(x.shape, x.dtype),
        scratch_shapes=[pltpu.VMEM((RET_HEADS, RET_QK_DIM, RET_V_DIM), F32),
                        pltpu.VMEM((t, d), BF16),
                        pltpu.VMEM((2 * d // SLAB, d, SLAB), BF16)],
        compiler_params=pltpu.CompilerParams(
            dimension_semantics=("arbitrary", "arbitrary"),
            vmem_limit_bytes=VMEM_LIMIT_BYTES),
        name="retention_layer",
    )(x, x.reshape(b * n_tiles, t, d), g.reshape(1, d),
      *([w_in] * RET_IN_BLOCKS), *([w_out] * RET_OUT_BLOCKS),
      perm, cos, sin, idec, qdec, kdec, cdec)


def _retention_tables(s):
    dk = RET_QK_DIM
    angle = 1.0 / (10000.0 ** jnp.linspace(0.0, 1.0, dk // 2, dtype=F32))
    pos = jnp.arange(s, dtype=F32)[:, None]
    sin, cos = jnp.sin(pos * angle), jnp.cos(pos * angle)
    c = RET_CHUNK
    log_gamma = jnp.log(1.0 - 2.0 ** (-5.0 - jnp.arange(RET_HEADS, dtype=F32)))
    idx = jnp.arange(c, dtype=F32)
    diff = idx[:, None] - idx[None, :]
    idec = jnp.where(diff[None] >= 0,
                     jnp.exp(jnp.maximum(diff, 0.0)[None] * log_gamma[:, None, None]), 0.0)
    qdec = jnp.exp((idx + 1.0)[None, :, None] * log_gamma[:, None, None])
    kdec = jnp.exp((c - 1.0 - idx)[None, :, None] * log_gamma[:, None, None])
    cdec = jnp.exp(c * log_gamma)[:, None, None]
    return cos, sin, idec.astype(F32), qdec, kdec, cdec


def _kv_kernel(x_ref, g_ref, w_ref, gk_ref, k_ref, vt_ref):
    h = (_rms(x_ref[0]) * g_ref[...]).astype(BF16)
    kv = _dot(h, w_ref[...])
    k, v = kv[:, :KV_WIDTH], kv[:, KV_WIDTH:]
    lo = lax.broadcasted_iota(jnp.int32, (1, KV_WIDTH), 1) < SWA_HEAD_DIM
    sq = k * k
    ms_lo = jnp.sum(jnp.where(lo, sq, 0.0), axis=-1, keepdims=True) / SWA_HEAD_DIM
    ms_hi = jnp.sum(jnp.where(lo, 0.0, sq), axis=-1, keepdims=True) / SWA_HEAD_DIM
    r = jnp.where(lo, lax.rsqrt(ms_lo + EPS), lax.rsqrt(ms_hi + EPS))
    k_ref[0] = k * r * gk_ref[...]
    vt_ref[0] = v.T


def _shared_kv(x, g, w_kv, g_k):
    b, s, d = x.shape
    t = KV_TILE
    return pl.pallas_call(
        _kv_kernel,
        grid=(b, s // t),
        in_specs=[
            pl.BlockSpec((1, t, d), lambda i, j: (i, j, 0)),
            _resident((1, d)),
            _resident(w_kv.shape),
            _resident((1, KV_WIDTH)),
        ],
        out_specs=[pl.BlockSpec((1, t, KV_WIDTH), lambda i, j: (i, j, 0)),
                   pl.BlockSpec((1, KV_WIDTH, t), lambda i, j: (i, 0, j))],
        out_shape=[jax.ShapeDtypeStruct((b, s, KV_WIDTH), F32),
                   jax.ShapeDtypeStruct((b, KV_WIDTH, s), F32)],
        compiler_params=pltpu.CompilerParams(
            dimension_semantics=("parallel", "parallel"),
            vmem_limit_bytes=VMEM_LIMIT_BYTES),
        name="shared_kv",
    )(x, g.reshape(1, d), w_kv.astype(BF16), jnp.tile(g_k, SWA_KV_HEADS).reshape(1, KV_WIDTH))


def _prev_key_valid():
    j = lax.broadcasted_iota(jnp.int32, (BLOCK, BLOCK), 0)
    i = lax.broadcasted_iota(jnp.int32, (BLOCK, BLOCK), 1)
    return j > i, j, i


def _bias_kernel(rel_ref, o_ref):
    hd = pl.program_id(0)
    prev, j, i = _prev_key_valid()
    dist = jnp.where(prev, i + BLOCK - j, i - j)
    max_exact = REL_BUCKETS // 2
    dist_f = jnp.maximum(dist, 1).astype(F32)
    large = max_exact + jnp.floor(jnp.log(dist_f / max_exact) / math.log(REL_MAX_DIST / max_exact)
                                  * (REL_BUCKETS - max_exact)).astype(jnp.int32)
    large = jnp.minimum(large, REL_BUCKETS - 1)
    bucket = jnp.where(dist < max_exact, dist, large)
    bias = jnp.zeros((BLOCK, BLOCK), F32)
    for bk in range(REL_BUCKETS):
        bias = jnp.where(bucket == bk, rel_ref[bk, hd], bias)
    o_ref[0, 0] = bias
    o_ref[1, 0] = jnp.where(prev, jnp.float32(-jnp.inf), bias)


def _bias_tables(rel_bias):
    return pl.pallas_call(
        _bias_kernel,
        grid=(SWA_Q_HEADS,),
        in_specs=[pl.BlockSpec(memory_space=pltpu.SMEM)],
        out_specs=pl.BlockSpec((2, 1, BLOCK, BLOCK), lambda h: (0, h, 0, 0)),
        out_shape=jax.ShapeDtypeStruct((2, SWA_Q_HEADS, BLOCK, BLOCK), F32),
        name="rel_bias_tables",
    )(rel_bias)


SWA_GROUPS = SWA_Q_HEADS // SWA_GROUP
SWA_IN_BLOCKS = 2 * SWA_Q_WIDTH // SLAB
SWA_OUT_BLOCKS = SWA_GROUPS * (D_MODEL // SLAB)
assert SWA_GROUP * SWA_HEAD_DIM == SLAB


def _swa_kernel(sinks_ref, x_ref, xnext_ref, g_ref, gq_ref, *refs):
    w_in = refs[:SWA_IN_BLOCKS]
    w_out = refs[SWA_IN_BLOCKS:SWA_IN_BLOCKS + SWA_OUT_BLOCKS]
    (kc_ref, kp_ref, vtc_ref, vtp_ref, bias_ref,
     o_ref, h_ref, q0_ref) = refs[SWA_IN_BLOCKS + SWA_OUT_BLOCKS:]
    out_per_group = D_MODEL // SLAB

    def normed(ref):
        return (_rms(ref[0]) * g_ref[...]).astype(BF16)

    @pl.when((pl.program_id(0) == 0) & (pl.program_id(1) == 0))
    def _():
        h_ref[...] = normed(x_ref)
        q0_ref[...] = _dot(h_ref[...], w_in[0][...])

    def prefetch_h():
        h_ref[...] = normed(xnext_ref)

    def prefetch_q0():
        q0_ref[...] = _dot(h_ref[...], w_in[0][...])

    first = (pl.program_id(1) == 0).astype(jnp.int32)
    x = x_ref[0]
    n_blocks = x.shape[0] // BLOCK
    n_groups = SWA_GROUPS

    lo = lax.broadcasted_iota(jnp.int32, (1, LANES), 1) < SWA_HEAD_DIM
    lo_rows = lax.broadcasted_iota(jnp.int32, (LANES, 1), 0) < SWA_HEAD_DIM

    kall = jnp.concatenate([kp_ref[0], kc_ref[0]], axis=0)
    kswap = pltpu.roll(kall, SWA_HEAD_DIM, 1)
    kdup = [jnp.where(lo, kall, kswap).astype(BF16), jnp.where(lo, kswap, kall).astype(BF16)]
    vt = jnp.concatenate([vtp_ref[0], vtc_ref[0]], axis=1).astype(BF16)
    gq = gq_ref[...]
    prev_valid, _, _ = _prev_key_valid()

    heads_per_kv = SWA_Q_HEADS // SWA_KV_HEADS

    q_slab = [None] * n_groups
    gate_slab = [None] * n_groups
    y_blocks = [[None] * n_blocks for _ in range(n_groups)]

    def project_q(grp):
        q_slab[grp] = _dot(h_ref[...], w_in[grp][...])

    def project_gate(grp):
        gate_slab[grp] = _dot(h_ref[...], w_in[n_groups + grp][...])

    def scores(blk, grp):
        band = slice(blk * BLOCK, (blk + 2) * BLOCK)
        kv = (SWA_GROUP * grp) // heads_per_kv
        tiles = []
        for pair in range(2):
            qp = q_slab[grp][blk * BLOCK:(blk + 1) * BLOCK, pair * LANES:(pair + 1) * LANES]
            qt = qp.T
            sq = qt * qt
            ms_lo = jnp.sum(sq[:SWA_HEAD_DIM], axis=0, keepdims=True) / SWA_HEAD_DIM
            ms_hi = jnp.sum(sq[SWA_HEAD_DIM:], axis=0, keepdims=True) / SWA_HEAD_DIM
            r = jnp.where(lo_rows, lax.rsqrt(ms_lo + EPS), lax.rsqrt(ms_hi + EPS))
            qt = qt * r * gq
            qstack = jnp.concatenate([jnp.where(lo_rows, qt, 0.0), jnp.where(lo_rows, 0.0, qt)],
                                     axis=1).astype(BF16)
            tiles.append(_dot(kdup[kv][band], qstack))
        return tiles

    def softmax(blk, grp, tiles):
        table = first if blk == 0 else 0
        pairs = []
        for pair in range(2):
            heads = []
            for pos in range(2):
                hd = SWA_GROUP * grp + 2 * pair + pos
                sink = sinks_ref[hd]
                cols = slice(pos * BLOCK, (pos + 1) * BLOCK)
                st = tiles[pair]
                s = (jnp.where(prev_valid, st[:BLOCK, cols], st[BLOCK:, cols])
                     + bias_ref[table, hd])
                m = jnp.maximum(jnp.max(s, axis=0, keepdims=True), sink)
                e = jnp.exp(s - m)
                den = jnp.sum(e, axis=0, keepdims=True) + jnp.exp(sink - m)
                pn = e * (1.0 / den)
                heads.append(jnp.concatenate([jnp.where(prev_valid, pn, 0.0),
                                              jnp.where(prev_valid, 0.0, pn)],
                                             axis=0).astype(BF16))
            pairs.append(jnp.concatenate(heads, axis=1))
        return pairs

    def finish(blk, grp, pairs):
        band = slice(blk * BLOCK, (blk + 2) * BLOCK)
        kv = (SWA_GROUP * grp) // heads_per_kv
        vt_band = vt[kv * SWA_HEAD_DIM:(kv + 1) * SWA_HEAD_DIM, band]
        ot = [_dot(vt_band, pairs[pair]) for pair in range(2)]
        ot = jnp.concatenate([ot[0][:, :BLOCK], ot[0][:, BLOCK:],
                              ot[1][:, :BLOCK], ot[1][:, BLOCK:]], axis=0)
        o = ot.T
        gate = gate_slab[grp][blk * BLOCK:(blk + 1) * BLOCK]
        y_blocks[grp][blk] = _gated(o, gate).astype(BF16)

    def out_project(grp):
        y = jnp.concatenate(y_blocks[grp], axis=0)
        part = _dot_cols(y, w_out[grp * out_per_group:(grp + 1) * out_per_group])
        if grp == 0:
            o_ref[0] = x + part
        else:
            o_ref[0] += part

    items = [(blk, grp) for grp in range(n_groups) for blk in range(n_blocks)]
    last = len(items) - 1
    jobs = []
    for grp in range(n_groups):
        jobs.append((functools.partial(project_gate, grp), 0, grp * n_blocks + 1))
        if grp + 1 < n_groups:
            jobs.append((functools.partial(project_q, grp + 1), 0, (grp + 1) * n_blocks - 2))
        if grp >= 1:
            jobs.append((functools.partial(out_project, grp - 1), grp * n_blocks + 1, last))
    jobs += [(prefetch_h, 0, last), (prefetch_q0, n_blocks - 1, last)]
    pairs = [jobs[n:n + 2] for n in range(0, len(jobs), 2)]
    fillers = {}
    slot = -1
    for n, pair in enumerate(pairs):
        slot = max(slot + 1, (n * len(items)) // len(pairs), *(first for _, first, _ in pair))
        assert all(slot <= last_item for _, _, last_item in pair), "filler after its consumer"
        fillers[slot] = [job for job, _, _ in pair]
    q_slab[0] = q0_ref
    pending = None
    for job in fillers.pop(0, []):
        job()
    tiles = scores(*items[0])
    for idx, (blk, grp) in enumerate(items):
        next_tiles = scores(*items[idx + 1]) if idx + 1 < len(items) else None
        for job in fillers.get(idx, []):
            job()
        if pending is not None:
            finish(*pending)
        pending = (blk, grp, softmax(blk, grp, tiles))
        tiles = next_tiles
    finish(*pending)
    out_project(n_groups - 1)


def _swa_layer(x, g, w_in, g_q, sinks, w_out, layer, k, vt, bias):
    b, s, d = x.shape
    t = SWA_TILE
    in_blocks = [_weight_block(layer, d, 0, c) for c in range(SWA_IN_BLOCKS)]
    out_blocks = [_weight_block(layer, SLAB, grp, c)
                  for grp in range(SWA_GROUPS) for c in range(d // SLAB)]

    def prev_block(j):
        return jnp.maximum(j * (t // BLOCK) - 1, 0)

    k_cur = pl.BlockSpec((1, t, KV_WIDTH), lambda i, j: (i, j, 0))
    k_prev = pl.BlockSpec((1, BLOCK, KV_WIDTH), lambda i, j: (i, prev_block(j), 0))
    vt_cur = pl.BlockSpec((1, KV_WIDTH, t), lambda i, j: (i, 0, j))
    vt_prev = pl.BlockSpec((1, KV_WIDTH, BLOCK), lambda i, j: (i, 0, prev_block(j)))
    n_tiles = s // t
    return pl.pallas_call(
        _swa_kernel,
        grid=(b, n_tiles),
        in_specs=[
            pl.BlockSpec(memory_space=pltpu.SMEM),
            pl.BlockSpec((1, t, d), lambda i, j: (i, j, 0)),
            _next_tile_spec(b, n_tiles, t, d),
            _resident((1, d)),
            _resident((LANES, BLOCK)),
            *in_blocks, *out_blocks,
            k_cur, k_prev, vt_cur, vt_prev,
            _resident(bias.shape),
        ],
        out_specs=pl.BlockSpec((1, t, d), lambda i, j: (i, j, 0)),
        out_shape=jax.ShapeDtypeStruct(x.shape, x.dtype),
        scratch_shapes=[pltpu.VMEM((t, d), BF16), pltpu.VMEM((t, SLAB), F32)],
        compiler_params=pltpu.CompilerParams(
            dimension_semantics=("arbitrary", "arbitrary"),
            vmem_limit_bytes=VMEM_LIMIT_BYTES),
        name="swa_layer",
    )(sinks, x, x.reshape(b * n_tiles, t, d), g.reshape(1, d),
      jnp.broadcast_to((jnp.tile(g_q, 2) * (SWA_HEAD_DIM ** -0.5))[:, None], (LANES, BLOCK)),
      *([w_in] * SWA_IN_BLOCKS), *([w_out] * SWA_OUT_BLOCKS), k, k, vt, vt, bias)


def kernel(x, a_norm_g, a_w_in, a_w_out, kv_norm_g, w_kv, k_norm_g, rel_bias,
           b_norm_g, b_w_in, b_q_norm_g, b_sinks, b_w_out):
    s = x.shape[1]
    tables = _retention_tables(s)
    a_w_in, a_w_out = a_w_in.astype(BF16), a_w_out.astype(BF16)
    b_w_in, b_w_out = b_w_in.astype(BF16), b_w_out.astype(BF16)
    for layer in range(a_w_in.shape[0]):
        x = _retention_layer(x, a_norm_g[layer], a_w_in, a_w_out, layer, *tables)
    k, vt = _shared_kv(x, kv_norm_g, w_kv, k_norm_g)
    bias = _bias_tables(rel_bias)
    for layer in range(b_w_in.shape[0]):
        x = _swa_layer(x, b_norm_g[layer], b_w_in, b_q_norm_g[layer], b_sinks[layer], b_w_out,
                       layer, k, vt, bias)
    return x
```

```python
import functools
import math

import jax
import jax.numpy as jnp
from jax import lax
from jax.experimental import pallas as pl
from jax.experimental.pallas import tpu as pltpu

D_MODEL = 1024
RET_HEADS = 4
RET_QK_DIM = D_MODEL // RET_HEADS
RET_V_DIM = 2 * RET_QK_DIM
RET_V_TOTAL = RET_HEADS * RET_V_DIM
RET_CHUNK = 128
A_IN_WIDTH = 2 * D_MODEL + 2 * RET_V_TOTAL
SWA_HEAD_DIM = 64
SWA_Q_HEADS = D_MODEL // SWA_HEAD_DIM
SWA_KV_HEADS = SWA_Q_HEADS // 8
SWA_Q_WIDTH = SWA_Q_HEADS * SWA_HEAD_DIM
KV_WIDTH = SWA_KV_HEADS * SWA_HEAD_DIM
WINDOW = 128
BLOCK = 128
REL_BUCKETS = 32
REL_MAX_DIST = 128
EPS = 1e-6

LANES = 128
SLAB = 256
VMEM_LIMIT_BYTES = 56 * 1024 * 1024

RET_TILE = 512
KV_TILE = 2048
SWA_TILE = 1024
SWA_GROUP = 4

BF16 = jnp.bfloat16
F32 = jnp.float32


def _dot(a, b):
    return jnp.dot(a, b, preferred_element_type=F32)


def _dot_nt(a, b):
    return lax.dot_general(a, b, (((1,), (1,)), ((), ())), preferred_element_type=F32)


def _dot_tn(a, b):
    return lax.dot_general(a, b, (((0,), (0,)), ((), ())), preferred_element_type=F32)


def _dot_cols(a, w_refs):
    return jnp.concatenate([_dot(a, w[...]) for w in w_refs], axis=1)


def _silu(x):
    return x * (0.5 * jnp.tanh(0.5 * x) + 0.5)


def _rms(x):
    return x * lax.rsqrt(jnp.mean(x * x, axis=-1, keepdims=True) + EPS)


def _resident(shape):
    nd = len(shape)
    return pl.BlockSpec(shape, lambda *_: (0,) * nd, pipeline_mode=pl.Buffered(1))


def _next_tile_spec(b, n_tiles, t, d):
    last = b * n_tiles - 1
    return pl.BlockSpec((1, t, d), lambda i, j: (jnp.minimum(i * n_tiles + j + 1, last), 0, 0))


def _weight_block(layer, rows, row_block, col_block):
    return pl.BlockSpec((None, rows, SLAB), lambda *_: (layer, row_block, col_block),
                        pipeline_mode=pl.Buffered(1))


RET_IN_BLOCKS = A_IN_WIDTH // SLAB
RET_OUT_BLOCKS = RET_HEADS * (D_MODEL // SLAB)


def _ret_kernel(x_ref, xnext_ref, g_ref, *refs):
    w_in = refs[:RET_IN_BLOCKS]
    w_out = refs[RET_IN_BLOCKS:RET_IN_BLOCKS + RET_OUT_BLOCKS]
    (perm_ref, cos_ref, sin_ref, idec_ref, qdec_ref, kdec_ref, cdec_ref,
     o_ref, state_ref, h_ref, wqk_ref) = refs[RET_IN_BLOCKS + RET_OUT_BLOCKS:]
    qk_blocks = D_MODEL // SLAB
    v_per_head = RET_V_DIM // SLAB
    out_per_head = D_MODEL // SLAB
    assert RET_QK_DIM == SLAB

    def wq(hd):
        return [wqk_ref.at[hd]]

    def wk(hd):
        return [wqk_ref.at[qk_blocks + hd]]

    def wv(hd):
        start = 2 * qk_blocks + hd * v_per_head
        return w_in[start:start + v_per_head]

    def wgate(hd):
        start = 2 * qk_blocks + RET_HEADS * v_per_head + hd * v_per_head
        return w_in[start:start + v_per_head]

    @pl.when(pl.program_id(1) == 0)
    def _():
        state_ref[...] = jnp.zeros_like(state_ref)

    def normed(ref):
        return (_rms(ref[0]) * g_ref[...]).astype(BF16)

    @pl.when((pl.program_id(0) == 0) & (pl.program_id(1) == 0))
    def _():
        h_ref[...] = normed(x_ref)
        for blk in range(2 * qk_blocks):
            wqk_ref[blk] = _dot(w_in[blk][...], perm_ref[...]).astype(BF16)

    def prefetch_h():
        h_ref[...] = normed(xnext_ref)

    cos = cos_ref[...]
    sin = sin_ref[...]
    half = RET_QK_DIM // 2
    n_chunks = x_ref.shape[1] // RET_CHUNK

    def rotate(t):
        t1, t2 = t[:, :half], t[:, half:]
        return jnp.concatenate([t1 * cos - t2 * sin, t2 * cos + t1 * sin], axis=1)

    proj = [dict() for _ in range(RET_HEADS)]

    def project_q(hd):
        proj[hd]["q"] = rotate(_dot_cols(h_ref[...], wq(hd))).astype(BF16)

    def project_k(hd):
        k = rotate(_dot_cols(h_ref[...], wk(hd))) * (RET_QK_DIM ** -0.5)
        proj[hd]["k"] = k.astype(BF16)
        kdec = jnp.concatenate([kdec_ref[hd]] * n_chunks, axis=0)
        proj[hd]["kd"] = (k * kdec).astype(BF16)

    def project_v(hd):
        proj[hd]["v"] = _dot_cols(h_ref[...], wv(hd)).astype(BF16)

    def project_gate(hd):
        proj[hd]["gate"] = _dot_cols(h_ref[...], wgate(hd))

    def finish(hd, outs):
        o = _rms(jnp.concatenate(outs, axis=0))
        gate = proj[hd]["gate"]
        y = (o * _silu(gate)).astype(BF16)
        part = _dot_cols(y, w_out[hd * out_per_head:(hd + 1) * out_per_head])
        if hd == 0:
            o_ref[0] = x_ref[0] + part
        else:
            o_ref[0] += part

    project_q(0)
    project_k(0)
    project_v(0)
    pending = None
    for hd in range(RET_HEADS):
        fillers = [functools.partial(project_gate, hd)]
        if hd + 1 < RET_HEADS:
            fillers += [functools.partial(f, hd + 1) for f in (project_q, project_k, project_v)]
        else:
            fillers += [prefetch_h]
        qb, kb, kd, v = proj[hd]["q"], proj[hd]["k"], proj[hd]["kd"], proj[hd]["v"]
        st = state_ref[hd]
        outs = []
        for c in range(n_chunks):
            rows = slice(c * RET_CHUNK, (c + 1) * RET_CHUNK)
            qc, kc, vc = qb[rows], kb[rows], v[rows]
            scores = _dot_nt(qc, kc) * idec_ref[hd]
            inter = _dot(qc, st.astype(BF16)) * qdec_ref[hd]
            st = st * cdec_ref[hd] + _dot_tn(kd[rows], vc)
            lo_f = (c * len(fillers)) // n_chunks
            hi_f = ((c + 1) * len(fillers)) // n_chunks
            for f in fillers[lo_f:hi_f]:
                f()
            if c == 0 and pending is not None:
                finish(*pending)
            outs.append(_dot(scores.astype(BF16), vc) + inter)
        state_ref[hd] = st
        pending = (hd, outs)
    finish(*pending)


def _retention_layer(x, g, w_in, w_out, layer, cos, sin, idec, qdec, kdec, cdec):
    b, s, d = x.shape
    t = RET_TILE
    in_blocks = [_weight_block(layer, d, 0, c) for c in range(RET_IN_BLOCKS)]
    out_blocks = [_weight_block(layer, RET_V_DIM, hd, c)
                  for hd in range(RET_HEADS) for c in range(d // SLAB)]
    n_tiles = s // t
    half = RET_QK_DIM // 2
    src = jnp.arange(RET_QK_DIM)
    dst = jnp.where(src % 2 == 0, src // 2, half + src // 2)
    perm = (dst[:, None] == jnp.arange(RET_QK_DIM)[None, :]).astype(BF16)
    return pl.pallas_call(
        _ret_kernel,
        grid=(b, n_tiles),
        in_specs=[
            pl.BlockSpec((1, t, d), lambda i, j: (i, j, 0)),
            _next_tile_spec(b, n_tiles, t, d),
            _resident((1, d)),
            *in_blocks, *out_blocks,
            _resident(perm.shape),
            pl.BlockSpec((t, half), lambda i, j: (j, 0)),
            pl.BlockSpec((t, half), lambda i, j: (j, 0)),
            _resident(idec.shape), _resident(qdec.shape), _resident(kdec.shape),
            _resident(cdec.shape),
        ],
        out_specs=pl.BlockSpec((1, t, d), lambda i, j: (i, j, 0)),
        out_shape=jax.ShapeDtypeStruct(x.shape, x.dtype),
        scratch_shapes=[pltpu.VMEM((RET_HEADS, RET_QK_DIM, RET_V_DIM), F32),
                        pltpu.VMEM((t, d), BF16),
                        pltpu.VMEM((2 * d // SLAB, d, SLAB), BF16)],
        compiler_params=pltpu.CompilerParams(
            dimension_semantics=("arbitrary", "arbitrary"),
            vmem_limit_bytes=VMEM_LIMIT_BYTES),
        name="retention_layer",
    )(x, x.reshape(b * n_tiles, t, d), g.reshape(1, d),
      *([w_in] * RET_IN_BLOCKS), *([w_out] * RET_OUT_BLOCKS),
      perm, cos, sin, idec, qdec, kdec, cdec)


def _retention_tables(s):
    dk = RET_QK_DIM
    angle = 1.0 / (10000.0 ** jnp.linspace(0.0, 1.0, dk // 2, dtype=F32))
    pos = jnp.arange(s, dtype=F32)[:, None]
    sin, cos = jnp.sin(pos * angle), jnp.cos(pos * angle)
    c = RET_CHUNK
    log_gamma = jnp.log(1.0 - 2.0 ** (-5.0 - jnp.arange(RET_HEADS, dtype=F32)))
    idx = jnp.arange(c, dtype=F32)
    diff = idx[:, None] - idx[None, :]
    idec = jnp.where(diff[None] >= 0,
                     jnp.exp(jnp.maximum(diff, 0.0)[None] * log_gamma[:, None, None]), 0.0)
    qdec = jnp.exp((idx + 1.0)[None, :, None] * log_gamma[:, None, None])
    kdec = jnp.exp((c - 1.0 - idx)[None, :, None] * log_gamma[:, None, None])
    cdec = jnp.exp(c * log_gamma)[:, None, None]
    return cos, sin, idec.astype(F32), qdec, kdec, cdec


def _kv_kernel(x_ref, g_ref, w_ref, gk_ref, k_ref, vt_ref):
    h = (_rms(x_ref[0]) * g_ref[...]).astype(BF16)
    kv = _dot(h, w_ref[...])
    k, v = kv[:, :KV_WIDTH], kv[:, KV_WIDTH:]
    lo = lax.broadcasted_iota(jnp.int32, (1, KV_WIDTH), 1) < SWA_HEAD_DIM
    sq = k * k
    ms_lo = jnp.sum(jnp.where(lo, sq, 0.0), axis=-1, keepdims=True) / SWA_HEAD_DIM
    ms_hi = jnp.sum(jnp.where(lo, 0.0, sq), axis=-1, keepdims=True) / SWA_HEAD_DIM
    r = jnp.where(lo, lax.rsqrt(ms_lo + EPS), lax.rsqrt(ms_hi + EPS))
    k_ref[0] = k * r * gk_ref[...]
    vt_ref[0] = v.T


def _shared_kv(x, g, w_kv, g_k):
    b, s, d = x.shape
    t = KV_TILE
    return pl.pallas_call(
        _kv_kernel,
        grid=(b, s // t),
        in_specs=[
            pl.BlockSpec((1, t, d), lambda i, j: (i, j, 0)),
            _resident((1, d)),
            _resident(w_kv.shape),
            _resident((1, KV_WIDTH)),
        ],
        out_specs=[pl.BlockSpec((1, t, KV_WIDTH), lambda i, j: (i, j, 0)),
                   pl.BlockSpec((1, KV_WIDTH, t), lambda i, j: (i, 0, j))],
        out_shape=[jax.ShapeDtypeStruct((b, s, KV_WIDTH), F32),
                   jax.ShapeDtypeStruct((b, KV_WIDTH, s), F32)],
        compiler_params=pltpu.CompilerParams(
            dimension_semantics=("parallel", "parallel"),
            vmem_limit_bytes=VMEM_LIMIT_BYTES),
        name="shared_kv",
    )(x, g.reshape(1, d), w_kv.astype(BF16), jnp.tile(g_k, SWA_KV_HEADS).reshape(1, KV_WIDTH))


def _prev_key_valid():
    j = lax.broadcasted_iota(jnp.int32, (BLOCK, BLOCK), 0)
    i = lax.broadcasted_iota(jnp.int32, (BLOCK, BLOCK), 1)
    return j > i, j, i


def _bias_kernel(rel_ref, o_ref):
    hd = pl.program_id(0)
    prev, j, i = _prev_key_valid()
    dist = jnp.where(prev, i + BLOCK - j, i - j)
    max_exact = REL_BUCKETS // 2
    dist_f = jnp.maximum(dist, 1).astype(F32)
    large = max_exact + jnp.floor(jnp.log(dist_f / max_exact) / math.log(REL_MAX_DIST / max_exact)
                                  * (REL_BUCKETS - max_exact)).astype(jnp.int32)
    large = jnp.minimum(large, REL_BUCKETS - 1)
    bucket = jnp.where(dist < max_exact, dist, large)
    bias = jnp.zeros((BLOCK, BLOCK), F32)
    for bk in range(REL_BUCKETS):
        bias = jnp.where(bucket == bk, rel_ref[bk, hd], bias)
    o_ref[0, 0] = bias
    o_ref[1, 0] = jnp.where(prev, jnp.float32(-jnp.inf), bias)


def _bias_tables(rel_bias):
    return pl.pallas_call(
        _bias_kernel,
        grid=(SWA_Q_HEADS,),
        in_specs=[pl.BlockSpec(memory_space=pltpu.SMEM)],
        out_specs=pl.BlockSpec((2, 1, BLOCK, BLOCK), lambda h: (0, h, 0, 0)),
        out_shape=jax.ShapeDtypeStruct((2, SWA_Q_HEADS, BLOCK, BLOCK), F32),
        name="rel_bias_tables",
    )(rel_bias)


SWA_GROUPS = SWA_Q_HEADS // SWA_GROUP
SWA_IN_BLOCKS = 2 * SWA_Q_WIDTH // SLAB
SWA_OUT_BLOCKS = SWA_GROUPS * (D_MODEL // SLAB)
assert SWA_GROUP * SWA_HEAD_DIM == SLAB


def _swa_kernel(sinks_ref, x_ref, xnext_ref, g_ref, gq_ref, *refs):
    w_in = refs[:SWA_IN_BLOCKS]
    w_out = refs[SWA_IN_BLOCKS:SWA_IN_BLOCKS + SWA_OUT_BLOCKS]
    (kc_ref, kp_ref, vtc_ref, vtp_ref, bias_ref,
     o_ref, h_ref, q0_ref) = refs[SWA_IN_BLOCKS + SWA_OUT_BLOCKS:]
    out_per_group = D_MODEL // SLAB

    def normed(ref):
        return (_rms(ref[0]) * g_ref[...]).astype(BF16)

    @pl.when((pl.program_id(0) == 0) & (pl.program_id(1) == 0))
    def _():
        h_ref[...] = normed(x_ref)
        q0_ref[...] = _dot(h_ref[...], w_in[0][...])

    def prefetch_h():
        h_ref[...] = normed(xnext_ref)

    def prefetch_q0():
        q0_ref[...] = _dot(h_ref[...], w_in[0][...])

    first = (pl.program_id(1) == 0).astype(jnp.int32)
    x = x_ref[0]
    n_blocks = x.shape[0] // BLOCK
    n_groups = SWA_GROUPS

    lo = lax.broadcasted_iota(jnp.int32, (1, LANES), 1) < SWA_HEAD_DIM
    lo_rows = lax.broadcasted_iota(jnp.int32, (LANES, 1), 0) < SWA_HEAD_DIM

    kall = jnp.concatenate([kp_ref[0], kc_ref[0]], axis=0)
    kswap = pltpu.roll(kall, SWA_HEAD_DIM, 1)
    kdup = [jnp.where(lo, kall, kswap).astype(BF16), jnp.where(lo, kswap, kall).astype(BF16)]
    vt = jnp.concatenate([vtp_ref[0], vtc_ref[0]], axis=1).astype(BF16)
    gq = gq_ref[...]
    prev_valid, _, _ = _prev_key_valid()

    heads_per_kv = SWA_Q_HEADS // SWA_KV_HEADS

    q_slab = [None] * n_groups
    gate_slab = [None] * n_groups
    y_blocks = [[None] * n_blocks for _ in range(n_groups)]

    def project_q(grp):
        q_slab[grp] = _dot(h_ref[...], w_in[grp][...])

    def project_gate(grp):
        gate_slab[grp] = _dot(h_ref[...], w_in[n_groups + grp][...])

    def scores(blk, grp):
        band = slice(blk * BLOCK, (blk + 2) * BLOCK)
        kv = (SWA_GROUP * grp) // heads_per_kv
        tiles = []
        for pair in range(2):
            qp = q_slab[grp][blk * BLOCK:(blk + 1) * BLOCK, pair * LANES:(pair + 1) * LANES]
            qt = qp.T
            sq = qt * qt
            ms_lo = jnp.sum(sq[:SWA_HEAD_DIM], axis=0, keepdims=True) / SWA_HEAD_DIM
            ms_hi = jnp.sum(sq[SWA_HEAD_DIM:], axis=0, keepdims=True) / SWA_HEAD_DIM
            r = jnp.where(lo_rows, lax.rsqrt(ms_lo + EPS), lax.rsqrt(ms_hi + EPS))
            qt = qt * r * gq
            qstack = jnp.concatenate([jnp.where(lo_rows, qt, 0.0), jnp.where(lo_rows, 0.0, qt)],
                                     axis=1).astype(BF16)
            tiles.append(_dot(kdup[kv][band], qstack))
        return tiles

    def softmax(blk, grp, tiles):
        table = first if blk == 0 else 0
        pairs = []
        for pair in range(2):
            heads = []
            for pos in range(2):
                hd = SWA_GROUP * grp + 2 * pair + pos
                sink = sinks_ref[hd]
                cols = slice(pos * BLOCK, (pos + 1) * BLOCK)
                st = tiles[pair]
                s = (jnp.where(prev_valid, st[:BLOCK, cols], st[BLOCK:, cols])
                     + bias_ref[table, hd])
                m = jnp.maximum(jnp.max(s, axis=0, keepdims=True), sink)
                e = jnp.exp(s - m)
                den = jnp.sum(e, axis=0, keepdims=True) + jnp.exp(sink - m)
                pn = e * (1.0 / den)
                heads.append(jnp.concatenate([jnp.where(prev_valid, pn, 0.0),
                                              jnp.where(prev_valid, 0.0, pn)],
                                             axis=0).astype(BF16))
            pairs.append(jnp.concatenate(heads, axis=1))
        return pairs

    def finish(blk, grp, pairs):
        band = slice(blk * BLOCK, (blk + 2) * BLOCK)
        kv = (SWA_GROUP * grp) // heads_per_kv
        vt_band = vt[kv * SWA_HEAD_DIM:(kv + 1) * SWA_HEAD_DIM, band]
        ot = [_dot(vt_band, pairs[pair]) for pair in range(2)]
        ot = jnp.concatenate([ot[0][:, :BLOCK], ot[0][:, BLOCK:],
                              ot[1][:, :BLOCK], ot[1][:, BLOCK:]], axis=0)
        o = ot.T
        gate = gate_slab[grp][blk * BLOCK:(blk + 1) * BLOCK]
        y_blocks[grp][blk] = (o * _silu(gate)).astype(BF16)

    def out_project(grp):
        y = jnp.concatenate(y_blocks[grp], axis=0)
        part = _dot_cols(y, w_out[grp * out_per_group:(grp + 1) * out_per_group])
        if grp == 0:
            o_ref[0] = x + part
        else:
            o_ref[0] += part

    items = [(blk, grp) for grp in range(n_groups) for blk in range(n_blocks)]
    last = len(items) - 1
    jobs = []
    for grp in range(n_groups):
        jobs.append((functools.partial(project_gate, grp), 0, grp * n_blocks + 1))
        if grp + 1 < n_groups:
            jobs.append((functools.partial(project_q, grp + 1), 0, (grp + 1) * n_blocks - 2))
        if grp >= 1:
            jobs.append((functools.partial(out_project, grp - 1), grp * n_blocks + 1, last))
    jobs += [(prefetch_h, 0, last), (prefetch_q0, n_blocks - 1, last)]
    pairs = [jobs[n:n + 2] for n in range(0, len(jobs), 2)]
    fillers = {}
    slot = -1
    for n, pair in enumerate(pairs):
        slot = max(slot + 1, (n * len(items)) // len(pairs), *(first for _, first, _ in pair))
        assert all(slot <= last_item for _, _, last_item in pair), "filler after its consumer"
        fillers[slot] = [job for job, _, _ in pair]
    q_slab[0] = q0_ref
    pending = None
    for job in fillers.pop(0, []):
        job()
    tiles = scores(*items[0])
    for idx, (blk, grp) in enumerate(items):
        next_tiles = scores(*items[idx + 1]) if idx + 1 < len(items) else None
        for job in fillers.get(idx, []):
            job()
        if pending is not None:
            finish(*pending)
        pending = (blk, grp, softmax(blk, grp, tiles))
        tiles = next_tiles
    finish(*pending)
    out_project(n_groups - 1)


def _swa_layer(x, g, w_in, g_q, sinks, w_out, layer, k, vt, bias):
    b, s, d = x.shape
    t = SWA_TILE
    in_blocks = [_weight_block(layer, d, 0, c) for c in range(SWA_IN_BLOCKS)]
    out_blocks = [_weight_block(layer, SLAB, grp, c)
                  for grp in range(SWA_GROUPS) for c in range(d // SLAB)]

    def prev_block(j):
        return jnp.maximum(j * (t // BLOCK) - 1, 0)

    k_cur = pl.BlockSpec((1, t, KV_WIDTH), lambda i, j: (i, j, 0))
    k_prev = pl.BlockSpec((1, BLOCK, KV_WIDTH), lambda i, j: (i, prev_block(j), 0))
    vt_cur = pl.BlockSpec((1, KV_WIDTH, t), lambda i, j: (i, 0, j))
    vt_prev = pl.BlockSpec((1, KV_WIDTH, BLOCK), lambda i, j: (i, 0, prev_block(j)))
    n_tiles = s // t
    return pl.pallas_call(
        _swa_kernel,
        grid=(b, n_tiles),
        in_specs=[
            pl.BlockSpec(memory_space=pltpu.SMEM),
            pl.BlockSpec((1, t, d), lambda i, j: (i, j, 0)),
            _next_tile_spec(b, n_tiles, t, d),
            _resident((1, d)),
            _resident((LANES, BLOCK)),
            *in_blocks, *out_blocks,
            k_cur, k_prev, vt_cur, vt_prev,
            _resident(bias.shape),
        ],
        out_specs=pl.BlockSpec((1, t, d), lambda i, j: (i, j, 0)),
        out_shape=jax.ShapeDtypeStruct(x.shape, x.dtype),
        scratch_shapes=[pltpu.VMEM((t, d), BF16), pltpu.VMEM((t, SLAB), F32)],
        compiler_params=pltpu.CompilerParams(
            dimension_semantics=("arbitrary", "arbitrary"),
            vmem_limit_bytes=VMEM_LIMIT_BYTES),
        name="swa_layer",
    )(sinks, x, x.reshape(b * n_tiles, t, d), g.reshape(1, d),
      jnp.broadcast_to((jnp.tile(g_q, 2) * (SWA_HEAD_DIM ** -0.5))[:, None], (LANES, BLOCK)),
      *([w_in] * SWA_IN_BLOCKS), *([w_out] * SWA_OUT_BLOCKS), k, k, vt, vt, bias)


def kernel(x, a_norm_g, a_w_in, a_w_out, kv_norm_g, w_kv, k_norm_g, rel_bias,
           b_norm_g, b_w_in, b_q_norm_g, b_sinks, b_w_out):
    s = x.shape[1]
    tables = _retention_tables(s)
    a_w_in, a_w_out = a_w_in.astype(BF16), a_w_out.astype(BF16)
    b_w_in, b_w_out = b_w_in.astype(BF16), b_w_out.astype(BF16)
    for layer in range(a_w_in.shape[0]):
        x = _retention_layer(x, a_norm_g[layer], a_w_in, a_w_out, layer, *tables)
    k, vt = _shared_kv(x, kv_norm_g, w_kv, k_norm_g)
    bias = _bias_tables(rel_bias)
    for layer in range(b_w_in.shape[0]):
        x = _swa_layer(x, b_norm_g[layer], b_w_in, b_q_norm_g[layer], b_sinks[layer], b_w_out,
                       layer, k, vt, bias)
    return x
```
